```python
import math
import jax, jax.numpy as jnp
from jax import lax
import numpy as np


D_MODEL = 1024
BATCH = 8
SEQ = 2048
DEPTH = 1
DEC_BATCH = 2
DEC_SEQ = 8192
PAST_LEN = 128

D_SSM = 512
SSM_GROUP = 16
N_SSM_GROUPS = D_SSM // SSM_GROUP
STATE_P = 64
N_HEADS = 8
HEAD_DIM = 64
D_ATTN = N_HEADS * HEAD_DIM
D_MIX = D_SSM + D_ATTN
SPLITS = (D_SSM, D_SSM, D_ATTN, D_ATTN, D_ATTN, D_ATTN)
D_IN_PROJ = sum(SPLITS)
GRID_W = 64
WIN_H = 8
WIN_W = 16
Q_BLOCK_W = 16
K_BLOCK_W = 32
N_COL_BLOCKS = GRID_W // Q_BLOCK_W
DT_MIN = 1e-3
DT_MAX = 1e-1
EPS = 1e-6

kernel_name = 'hymba_s5_natten_bidir_encoder'


def rmsnorm(x, g):
    x32 = x.astype(jnp.float32)
    y = x32 * lax.rsqrt(jnp.mean(x32 * x32, axis=-1, keepdims=True) + EPS)
    return (y * g.astype(jnp.float32)).astype(x.dtype)


def _linear_scan(a, b):
    def combine(left, right):
        a1, b1 = left
        a2, b2 = right
        return a1 * a2, a2 * b1 + b2
    _, h = lax.associative_scan(combine, (a, b), axis=1)
    return h


def s5_direction(u_c, lam_re, lam_im, b_re, b_im, c_re, c_im, log_dt):
    f32 = jnp.float32
    lam = lax.complex(lam_re.astype(f32), lam_im.astype(f32))
    dt = jnp.exp(log_dt.astype(f32))[:, None]
    a_bar = jnp.exp(lam * dt)
    b = lax.complex(b_re.astype(f32), b_im.astype(f32))
    b_bar = ((a_bar - 1.0) / lam)[..., None] * b
    bu = jnp.einsum('gpc,blgc->blgp', b_bar, u_c)
    h = _linear_scan(jnp.broadcast_to(a_bar, bu.shape), bu)
    c = lax.complex(c_re.astype(f32), c_im.astype(f32))
    return jnp.einsum('gcp,blgp->blgc', c, h).real


def s5_mixer(u, lam_re, lam_im, b_re, b_im, c_re, c_im, log_dt, d_skip, w_glu, b_glu):
    bt, L, _ = u.shape
    u32 = u.astype(jnp.float32)
    u_c = u32.reshape(bt, L, N_SSM_GROUPS, SSM_GROUP).astype(jnp.complex64)
    y_f = s5_direction(u_c, lam_re[0], lam_im[0], b_re[0], b_im[0], c_re[0], c_im[0], log_dt[0])
    y_b = jnp.flip(s5_direction(jnp.flip(u_c, axis=1), lam_re[1], lam_im[1], b_re[1], b_im[1],
                                c_re[1], c_im[1], log_dt[1]), axis=1)
    y = (y_f + y_b).reshape(bt, L, D_SSM) + d_skip.astype(jnp.float32) * u32
    y = jax.nn.gelu(y)
    y = y * jax.nn.sigmoid(y @ w_glu.astype(jnp.float32) + b_glu.astype(jnp.float32))
    return y.astype(u.dtype)


def neighbourhood_attention(q, k, v, rpb):
    bt, L, H, dh = q.shape
    rows = L // GRID_W
    kh = min(WIN_H, rows)
    r = np.arange(rows)
    row_start = np.clip(r - kh // 2, 0, rows - kh)
    rows_idx = row_start[:, None] + np.arange(kh)[None, :]
    row_off = rows_idx - r[:, None]
    cb_start = np.clip(np.arange(N_COL_BLOCKS) * Q_BLOCK_W - WIN_W // 2, 0, GRID_W - K_BLOCK_W)
    col_idx = cb_start[:, None] + np.arange(K_BLOCK_W)[None, :]
    qcol = np.arange(GRID_W).reshape(N_COL_BLOCKS, Q_BLOCK_W)
    q_start = np.clip(qcol - WIN_W // 2, 0, GRID_W - WIN_W)
    kcol = col_idx[:, None, :]
    valid = (kcol >= q_start[..., None]) & (kcol < q_start[..., None] + WIN_W)
    col_off = kcol - qcol[..., None]
    ri = row_off + (WIN_H - 1)
    ci = np.clip(col_off, -(WIN_W - 1), WIN_W - 1) + (WIN_W - 1)
    qg = q.reshape(bt, rows, N_COL_BLOCKS, Q_BLOCK_W, H, dh)
    kg = k.reshape(bt, rows, GRID_W, H, dh)
    vg = v.reshape(bt, rows, GRID_W, H, dh)
    gr = rows_idx[:, None, :, None]
    gc = col_idx[None, :, None, :]
    k_blk = kg[:, gr, gc]
    v_blk = vg[:, gr, gc]
    s = jnp.einsum('brcqhd,brckwhd->brchqkw', qg, k_blk).astype(jnp.float32) * (dh ** -0.5)
    bias = rpb.astype(jnp.float32)[:, ri[:, None, None, :, None], ci[None, :, :, None, :]]
    s = s + jnp.transpose(bias, (1, 2, 0, 3, 4, 5))[None]
    s = jnp.where(valid[None, None, :, None, :, None, :], s, -jnp.inf)
    shp = s.shape
    p = jax.nn.softmax(s.reshape(shp[:-2] + (kh * K_BLOCK_W,)), axis=-1).reshape(shp)
    o = jnp.einsum('brchqkw,brckwhd->brcqhd', p, v_blk.astype(jnp.float32))
    return o.reshape(bt, L, H, dh).astype(q.dtype)


def hybrid_layer(x, norm_g, w_in, lam_re, lam_im, b_re, b_im, c_re, c_im, log_dt,
                 d_skip, w_glu, b_glu, rpb, ssm_out_g, attn_out_g, w_out):
    bt, L, _ = x.shape
    h = rmsnorm(x, norm_g)
    proj = h @ w_in
    cuts = list(np.cumsum(SPLITS)[:-1])
    u_s, z_s, q, k, v, z_a = jnp.split(proj, cuts, axis=-1)
    y_s = s5_mixer(u_s, lam_re, lam_im, b_re, b_im, c_re, c_im, log_dt, d_skip, w_glu, b_glu)
    y_s = rmsnorm(y_s, ssm_out_g) * jax.nn.silu(z_s)
    y_a = neighbourhood_attention(q.reshape(bt, L, N_HEADS, HEAD_DIM),
                                  k.reshape(bt, L, N_HEADS, HEAD_DIM),
                                  v.reshape(bt, L, N_HEADS, HEAD_DIM), rpb).reshape(bt, L, D_ATTN)
    y_a = rmsnorm(y_a, attn_out_g) * jax.nn.silu(z_a)
    mixed = jnp.concatenate([y_s, y_a], axis=-1)
    return x + (mixed @ w_out).astype(x.dtype)


def trunk(x, norm_g, w_in, lam_re, lam_im, b_re, b_im, c_re, c_im, log_dt,
          d_skip, w_glu, b_glu, rpb, ssm_out_g, attn_out_g, w_out, final_norm_g):
    for i in range(DEPTH):
        x = hybrid_layer(x, norm_g[i], w_in[i], lam_re[i], lam_im[i], b_re[i], b_im[i],
                         c_re[i], c_im[i], log_dt[i], d_skip[i], w_glu[i], b_glu[i],
                         rpb[i], ssm_out_g[i], attn_out_g[i], w_out[i])
    return rmsnorm(x, final_norm_g)


def setup_inputs(seed: int = 0) -> dict:
    key = jax.random.key(seed)
    ks = jax.random.split(key, 20)
    f32 = jnp.float32
    G, P, C = N_SSM_GROUPS, STATE_P, SSM_GROUP
    x_prompt = jax.random.normal(ks[0], (BATCH, SEQ, D_MODEL), f32)
    x_sample = jax.random.normal(ks[1], (DEC_BATCH, DEC_SEQ, D_MODEL), f32)
    norm_g = 1.0 + 0.02 * jax.random.normal(ks[2], (DEPTH, D_MODEL), f32)
    w_in = jax.random.normal(ks[3], (DEPTH, D_MODEL, D_IN_PROJ), f32) * D_MODEL ** -0.5
    lam_re = -0.5 + 0.01 * jax.random.normal(ks[4], (DEPTH, 2, G, P), f32)
    lam_im = (np.pi * jnp.arange(P, dtype=f32))[None, None, None, :] + 0.01 * jax.random.normal(ks[5], (DEPTH, 2, G, P), f32)
    b_re = jax.random.normal(ks[6], (DEPTH, 2, G, P, C), f32) * (2 * C) ** -0.5
    b_im = jax.random.normal(ks[7], (DEPTH, 2, G, P, C), f32) * (2 * C) ** -0.5
    c_re = jax.random.normal(ks[8], (DEPTH, 2, G, C, P), f32) * P ** -0.5
    c_im = jax.random.normal(ks[9], (DEPTH, 2, G, C, P), f32) * P ** -0.5
    log_dt = jax.random.uniform(ks[10], (DEPTH, 2, G), f32, math.log(DT_MIN), math.log(DT_MAX))
    d_skip = jax.random.normal(ks[11], (DEPTH, D_SSM), f32)
    w_glu = jax.random.normal(ks[12], (DEPTH, D_SSM, D_SSM), f32) * D_SSM ** -0.5
    b_glu = 0.01 * jax.random.normal(ks[13], (DEPTH, D_SSM), f32)
    rpb = 0.02 * jax.random.normal(ks[14], (DEPTH, N_HEADS, 2 * WIN_H - 1, 2 * WIN_W - 1), f32)
    ssm_out_g = 1.0 + 0.02 * jax.random.normal(ks[15], (DEPTH, D_SSM), f32)
    attn_out_g = 1.0 + 0.02 * jax.random.normal(ks[16], (DEPTH, D_ATTN), f32)
    w_out = jax.random.normal(ks[17], (DEPTH, D_MIX, D_MODEL), f32) * D_MIX ** -0.5
    final_norm_g = 1.0 + 0.02 * jax.random.normal(ks[18], (D_MODEL,), f32)
    return {'x_prompt': x_prompt, 'x_sample': x_sample, 'norm_g': norm_g, 'w_in': w_in,
            'lam_re': lam_re, 'lam_im': lam_im, 'b_re': b_re, 'b_im': b_im,
            'c_re': c_re, 'c_im': c_im, 'log_dt': log_dt, 'd_skip': d_skip,
            'w_glu': w_glu, 'b_glu': b_glu, 'rpb': rpb, 'ssm_out_g': ssm_out_g,
            'attn_out_g': attn_out_g, 'w_out': w_out, 'final_norm_g': final_norm_g}


def reference(x_prompt, x_sample, norm_g, w_in, lam_re, lam_im, b_re, b_im, c_re, c_im,
              log_dt, d_skip, w_glu, b_glu, rpb, ssm_out_g, attn_out_g, w_out, final_norm_g):
    y_prompt = trunk(x_prompt, norm_g, w_in, lam_re, lam_im, b_re, b_im, c_re, c_im, log_dt,
                     d_skip, w_glu, b_glu, rpb, ssm_out_g, attn_out_g, w_out, final_norm_g)
    y_sample = trunk(x_sample, norm_g, w_in, lam_re, lam_im, b_re, b_im, c_re, c_im, log_dt,
                     d_skip, w_glu, b_glu, rpb, ssm_out_g, attn_out_g, w_out, final_norm_g)
    return (y_prompt, y_sample)
```

```python
import functools

import jax
import jax.numpy as jnp
import numpy as np
from jax import lax
from jax.experimental import pallas as pl
from jax.experimental.pallas import tpu as pltpu

F32 = jnp.float32
BF16 = jnp.bfloat16
HI = lax.Precision.HIGHEST

D_MODEL = 1024
D_SSM = 512
SSM_GROUP = 16
N_GROUPS = D_SSM // SSM_GROUP
N_PAIRS = N_GROUPS // 2
STATE_P = 64
N_HEADS = 8
HEAD_DIM = 64
D_ATTN = N_HEADS * HEAD_DIM
D_IN_PROJ = 3072
GRID_W = 64
WIN_H = 8
WIN_W = 16
EPS = 1e-6
CHUNK = 16
CW = CHUNK * SSM_GROUP
NEG = -1e30

TOKEN_TILE = 512
VMEM_LIMIT = 56 * 1024 * 1024


def _rms(x, g):
    return x * lax.rsqrt(jnp.mean(x * x, axis=-1, keepdims=True) + EPS) * g


def _s5_tables(lam_re, lam_im, b_re, b_im, c_re, c_im, log_dt):
    T, G, P, C = CHUNK, N_GROUPS, STATE_P, SSM_GROUP
    dt = jnp.exp(log_dt)[..., None]
    xr, xi = lam_re * dt, lam_im * dt
    n = jnp.arange(T + 1, dtype=F32)[:, None, None, None]
    mag = jnp.exp(n * xr)
    pr, pi = mag * jnp.cos(n * xi), mag * jnp.sin(n * xi)
    a_re, a_im = pr[1], pi[1]
    den = lam_re * lam_re + lam_im * lam_im
    co_re = ((a_re - 1.0) * lam_re + a_im * lam_im) / den
    co_im = (a_im * lam_re - (a_re - 1.0) * lam_im) / den
    bb_re = co_re[..., None] * b_re - co_im[..., None] * b_im
    bb_im = co_re[..., None] * b_im + co_im[..., None] * b_re
    e_re = pr[..., None] * bb_re - pi[..., None] * bb_im
    e_im = pr[..., None] * bb_im + pi[..., None] * bb_re
    f_re = c_re[None] * pr[:, :, :, None, :] - c_im[None] * pi[:, :, :, None, :]
    f_im = c_re[None] * pi[:, :, :, None, :] + c_im[None] * pr[:, :, :, None, :]
    kk = (jnp.einsum('dgcp,ndgpe->ndgce', c_re, e_re[:T], precision=HI)
          - jnp.einsum('dgcp,ndgpe->ndgce', c_im, e_im[:T], precision=HI))
    jj = np.arange(T)[:, None]
    ii = np.arange(T)[None, :]
    lag_f = np.clip(ii - jj, 0, T - 1)
    lag_b = np.clip(jj - ii, 0, T - 1)
    m_f = jnp.asarray((jj <= ii).astype(np.float32))[:, :, None, None, None]
    m_b = jnp.asarray((jj >= ii).astype(np.float32))[:, :, None, None, None]
    toe = kk[lag_f, 0] * m_f + kk[lag_b, 1] * m_b
    mt = jnp.transpose(toe, (2, 0, 4, 1, 3)).reshape(G, CW, CW)

    def pairs(w):
        return w.reshape(N_PAIRS, 2, w.shape[1], w.shape[2])

    jrev = np.arange(T)[::-1]
    si_parts = [e_re[jrev, 0], e_im[jrev, 0], e_re[:T, 1], e_im[:T, 1]]
    si_parts = [pairs(jnp.transpose(w, (1, 0, 3, 2)).reshape(G, CW, P)) for w in si_parts]
    wsi = jnp.zeros((N_PAIRS, 2, CW, 4, 2, P), F32)
    for part, w in enumerate(si_parts):
        for gl in range(2):
            wsi = wsi.at[:, gl, :, part, gl, :].set(w[:, gl])
    wsi = wsi.reshape(N_PAIRS, 2 * CW, 8 * P)

    irev = T - np.arange(T)
    so_parts = [f_re[1:T + 1, 0], -f_im[1:T + 1, 0], f_re[irev, 1], -f_im[irev, 1]]
    so_parts = [pairs(jnp.transpose(w, (1, 3, 0, 2)).reshape(G, P, CW)) for w in so_parts]
    wso = jnp.zeros((N_PAIRS, 4, 2, P, 2, CW), F32)
    for part, w in enumerate(so_parts):
        for gl in range(2):
            wso = wso.at[:, part, gl, :, gl, :].set(w[:, gl])
    wso = wso.reshape(N_PAIRS, 8 * P, 2 * CW)

    at = jnp.stack([pr[T, 0], pi[T, 0], pr[T, 1], pi[T, 1]], axis=0)
    at = at.reshape(4, N_PAIRS, 2 * P).transpose(1, 0, 2).reshape(8, 8 * 2 * P)
    return mt.astype(BF16), wsi.astype(BF16), wso.astype(BF16), at


def _bias_table(rpb):
    qc = np.arange(GRID_W)[:, None]
    kc = np.arange(GRID_W)[None, :]
    q_start = np.clip(qc - WIN_W // 2, 0, GRID_W - WIN_W)
    valid = (kc >= q_start) & (kc < q_start + WIN_W)
    ci = np.clip(kc - qc, -(WIN_W - 1), WIN_W - 1) + (WIN_W - 1)
    ri = np.arange(WIN_H)[:, None] + np.arange(WIN_H)[None, :]
    b = rpb.astype(F32)[:, ri[:, :, None, None], ci[None, None]]
    b = jnp.where(jnp.asarray(valid)[None, None, None], b, NEG)
    b = jnp.transpose(b, (0, 1, 3, 2, 4)).reshape(N_HEADS, WIN_H, GRID_W, WIN_H * GRID_W)
    b = b.reshape(N_HEADS // 2, 2, WIN_H, GRID_W, WIN_H * GRID_W)
    return jnp.transpose(b, (0, 2, 1, 3, 4)).reshape(N_HEADS // 2, WIN_H, 2 * GRID_W, WIN_H * GRID_W)


def _in_proj_kernel(x_ref, g_ref, w_ref, o_ref):
    h = _rms(x_ref[...], g_ref[...]).astype(BF16)
    o_ref[...] = jnp.dot(h, w_ref[...], preferred_element_type=F32).astype(BF16)


def _state_in_kernel(u_ref, w_ref, s_ref):
    s_ref[...] = jnp.dot(u_ref[...], w_ref[0], preferred_element_type=F32)


def _scan_kernel(s_ref, a_ref, h_ref, *, cps):
    ar_f, ai_f = a_ref[:, 0:128], a_ref[:, 128:256]
    ar_b, ai_b = a_ref[:, 256:384], a_ref[:, 384:512]

    def body(k, carry):
        hfr, hfi, hbr, hbi = carry
        kb = cps - 1 - k
        h_ref[0, k, :, 0:128] = hfr
        h_ref[0, k, :, 128:256] = hfi
        h_ref[0, kb, :, 256:384] = hbr
        h_ref[0, kb, :, 384:512] = hbi
        nfr = ar_f * hfr - ai_f * hfi + s_ref[0, k, :, 0:128]
        nfi = ar_f * hfi + ai_f * hfr + s_ref[0, k, :, 128:256]
        nbr = ar_b * hbr - ai_b * hbi + s_ref[0, kb, :, 256:384]
        nbi = ar_b * hbi + ai_b * hbr + s_ref[0, kb, :, 384:512]
        return nfr, nfi, nbr, nbi

    z = jnp.zeros((8, 128), F32)
    lax.fori_loop(0, cps, body, (z, z, z, z))


def _s5_out_kernel(u_ref, h_ref, mt_ref, wso_ref, y_ref):
    u = u_ref[...]
    y = jnp.dot(h_ref[...].astype(BF16), wso_ref[0], preferred_element_type=F32)
    y0 = y[:, :CW] + jnp.dot(u[:, :CW], mt_ref[0], preferred_element_type=F32)
    y1 = y[:, CW:] + jnp.dot(u[:, CW:], mt_ref[1], preferred_element_type=F32)
    y_ref[:, :CW] = y0.astype(BF16)
    y_ref[:, CW:] = y1.astype(BF16)


def _attn_kernel(q_ref, k_ref, v_ref, b_ref, o_ref, *, rows):
    lane = lax.broadcasted_iota(jnp.int32, (GRID_W, 2 * HEAD_DIM), 1)
    first = lane < HEAD_DIM
    nkeys = WIN_H * GRID_W

    def body(r, carry):
        rs = jnp.clip(r - WIN_H // 2, 0, rows - WIN_H)
        ri0 = rs - r + (WIN_H - 1)
        q0 = pl.multiple_of(r * GRID_W, GRID_W)
        k0 = pl.multiple_of(rs * GRID_W, GRID_W)
        q = q_ref[pl.ds(q0, GRID_W), :] * jnp.asarray(HEAD_DIM ** -0.5, BF16)
        zero = jnp.zeros_like(q)
        q2 = jnp.concatenate([jnp.where(first, q, zero), jnp.where(first, zero, q)], axis=0)
        kw = k_ref[pl.ds(k0, nkeys), :]
        vw = v_ref[pl.ds(k0, nkeys), :]
        s = lax.dot_general(q2, kw, (((1,), (1,)), ((), ())), preferred_element_type=F32)
        s = s + b_ref[0, ri0]
        m = jnp.max(s, axis=-1, keepdims=True)
        p = jnp.exp(s - m)
        l = jnp.sum(p, axis=-1, keepdims=True)
        o2 = jnp.dot(p.astype(BF16), vw, preferred_element_type=F32) * (1.0 / l)
        o = jnp.where(first, o2[:GRID_W], o2[GRID_W:])
        o_ref[pl.ds(q0, GRID_W), :] = o.astype(BF16)
        return carry

    lax.fori_loop(0, rows, body, 0)


def _out_kernel(x_ref, y_ref, u_ref, zs_ref, o_ref, za_ref, dskip_ref, wglu_ref, bglu_ref,
                gs_ref, ga_ref, wout_ref, gfin_ref, out_ref):
    y = y_ref[...].astype(F32) + dskip_ref[...] * u_ref[...].astype(F32)
    y = jax.nn.gelu(y)
    gate = jnp.dot(y.astype(BF16), wglu_ref[...], preferred_element_type=F32) + bglu_ref[...]
    y = y * jax.nn.sigmoid(gate)
    ys = _rms(y, gs_ref[...]) * jax.nn.silu(zs_ref[...].astype(F32))
    ya = _rms(o_ref[...].astype(F32), ga_ref[...]) * jax.nn.silu(za_ref[...].astype(F32))
    mixed = jnp.concatenate([ys, ya], axis=-1).astype(BF16)
    out = x_ref[...] + jnp.dot(mixed, wout_ref[...], preferred_element_type=F32)
    out_ref[...] = _rms(out, gfin_ref[...])


def _params(**kw):
    return pltpu.CompilerParams(vmem_limit_bytes=VMEM_LIMIT, **kw)


def _trunk(x, tabs):
    (norm_g, w_in, mt, wsi, wso, at, bias, d_skip, w_glu, b_glu, gs, ga, w_out, gfin) = tabs
    bsz, seq, _ = x.shape
    n = bsz * seq
    nc = n // CHUNK
    cps = seq // CHUNK
    rows = seq // GRID_W
    tm = TOKEN_TILE
    x2 = x.reshape(n, D_MODEL)

    proj = pl.pallas_call(
        _in_proj_kernel,
        grid=(n // tm,),
        in_specs=[pl.BlockSpec((tm, D_MODEL), lambda i: (i, 0)),
                  pl.BlockSpec((1, D_MODEL), lambda i: (0, 0)),
                  pl.BlockSpec((D_MODEL, D_IN_PROJ), lambda i: (0, 0))],
        out_specs=pl.BlockSpec((tm, D_IN_PROJ), lambda i: (i, 0)),
        out_shape=jax.ShapeDtypeStruct((n, D_IN_PROJ), BF16),
        compiler_params=_params(),
        name="in_proj",
    )(x2, norm_g, w_in)

    u_c = proj[:, :D_SSM].reshape(nc, CHUNK, N_GROUPS, SSM_GROUP)
    u_c = jnp.transpose(u_c, (0, 2, 1, 3)).reshape(nc, N_GROUPS * CW)

    s_loc = pl.pallas_call(
        _state_in_kernel,
        grid=(N_PAIRS,),
        in_specs=[pl.BlockSpec((nc, 2 * CW), lambda p: (0, p)),
                  pl.BlockSpec((1, 2 * CW, 2 * CW), lambda p: (p, 0, 0))],
        out_specs=pl.BlockSpec((nc, 2 * CW), lambda p: (0, p)),
        out_shape=jax.ShapeDtypeStruct((nc, N_PAIRS * 2 * CW), F32),
        compiler_params=_params(),
        name="s5_state_in",
    )(u_c, wsi)

    h_in = pl.pallas_call(
        functools.partial(_scan_kernel, cps=cps),
        grid=(bsz, 2),
        in_specs=[pl.BlockSpec((1, cps, 8, 512), lambda b, m: (b, 0, 0, m)),
                  pl.BlockSpec((8, 512), lambda b, m: (0, m))],
        out_specs=pl.BlockSpec((1, cps, 8, 512), lambda b, m: (b, 0, 0, m)),
        out_shape=jax.ShapeDtypeStruct((bsz, cps, 8, 1024), F32),
        compiler_params=_params(),
        name="s5_scan",
    )(s_loc.reshape(bsz, cps, 8, 1024), at)

    y_c = pl.pallas_call(
        _s5_out_kernel,
        grid=(N_PAIRS,),
        in_specs=[pl.BlockSpec((nc, 2 * CW), lambda p: (0, p)),
                  pl.BlockSpec((nc, 2 * CW), lambda p: (0, p)),
                  pl.BlockSpec((2, CW, CW), lambda p: (p, 0, 0)),
                  pl.BlockSpec((1, 2 * CW, 2 * CW), lambda p: (p, 0, 0))],
        out_specs=pl.BlockSpec((nc, 2 * CW), lambda p: (0, p)),
        out_shape=jax.ShapeDtypeStruct((nc, N_PAIRS * 2 * CW), BF16),
        compiler_params=_params(),
        name="s5_out",
    )(u_c, h_in.reshape(nc, N_PAIRS * 2 * CW), mt, wso)

    y_s = jnp.transpose(y_c.reshape(nc, N_GROUPS, CHUNK, SSM_GROUP), (0, 2, 1, 3)).reshape(n, D_SSM)

    hp = 2 * HEAD_DIM
    o_attn = pl.pallas_call(
        functools.partial(_attn_kernel, rows=rows),
        grid=(N_HEADS // 2, bsz),
        in_specs=[pl.BlockSpec((seq, hp), lambda p, b: (b, 8 + p)),
                  pl.BlockSpec((seq, hp), lambda p, b: (b, 12 + p)),
                  pl.BlockSpec((seq, hp), lambda p, b: (b, 16 + p)),
                  pl.BlockSpec((1, WIN_H, 2 * GRID_W, WIN_H * GRID_W), lambda p, b: (p, 0, 0, 0))],
        out_specs=pl.BlockSpec((seq, hp), lambda p, b: (b, p)),
        out_shape=jax.ShapeDtypeStruct((n, D_ATTN), BF16),
        compiler_params=_params(),
        name="attention",
    )(proj, proj, proj, bias)

    vec = lambda width: pl.BlockSpec((1, width), lambda i: (0, 0))
    out = pl.pallas_call(
        _out_kernel,
        grid=(n // tm,),
        in_specs=[pl.BlockSpec((tm, D_MODEL), lambda i: (i, 0)),
                  pl.BlockSpec((tm, D_SSM), lambda i: (i, 0)),
                  pl.BlockSpec((tm, D_SSM), lambda i: (i, 0)),
                  pl.BlockSpec((tm, D_SSM), lambda i: (i, 1)),
                  pl.BlockSpec((tm, D_ATTN), lambda i: (i, 0)),
                  pl.BlockSpec((tm, D_ATTN), lambda i: (i, 5)),
                  vec(D_SSM),
                  pl.BlockSpec((D_SSM, D_SSM), lambda i: (0, 0)),
                  vec(D_SSM), vec(D_SSM), vec(D_ATTN),
                  pl.BlockSpec((D_MODEL, D_MODEL), lambda i: (0, 0)),
                  vec(D_MODEL)],
        out_specs=pl.BlockSpec((tm, D_MODEL), lambda i: (i, 0)),
        out_shape=jax.ShapeDtypeStruct((n, D_MODEL), F32),
        compiler_params=_params(),
        name="out_proj",
    )(x2, y_s, proj, proj, o_attn, proj, d_skip, w_glu, b_glu, gs, ga, w_out, gfin)
    return out.reshape(bsz, seq, D_MODEL)


def kernel(x_prompt, x_sample, norm_g, w_in, lam_re, lam_im, b_re, b_im, c_re, c_im, log_dt,
           d_skip, w_glu, b_glu, rpb, ssm_out_g, attn_out_g, w_out, final_norm_g):
    assert norm_g.shape[0] == 1, "single layer only"
    mt, wsi, wso, at = _s5_tables(lam_re[0], lam_im[0], b_re[0], b_im[0], c_re[0], c_im[0], log_dt[0])
    tabs = (norm_g[0][None], w_in[0].astype(BF16), mt, wsi, wso, at, _bias_table(rpb[0]),
            d_skip[0][None], w_glu[0].astype(BF16), b_glu[0][None], ssm_out_g[0][None],
            attn_out_g[0][None], w_out[0].astype(BF16), final_norm_g[None])
    return _trunk(x_prompt, tabs), _trunk(x_sample, tabs)
```

```python
import functools

import jax
import jax.numpy as jnp
import numpy as np
from jax import lax
from jax.experimental import pallas as pl
from jax.experimental.pallas import tpu as pltpu

F32 = jnp.float32
BF16 = jnp.bfloat16
HI = lax.Precision.HIGHEST

D_MODEL = 1024
D_SSM = 512
SSM_GROUP = 16
N_GROUPS = D_SSM // SSM_GROUP
N_PAIRS = N_GROUPS // 2
STATE_P = 64
N_HEADS = 8
HEAD_DIM = 64
D_ATTN = N_HEADS * HEAD_DIM
D_IN_PROJ = 3072
GRID_W = 64
WIN_H = 8
WIN_W = 16
EPS = 1e-6
CHUNK = 16
CW = CHUNK * SSM_GROUP
NEG = -1e30

TOKEN_TILE = 512
ATTN_ROWS_PER_STEP = 4
VMEM_LIMIT = 56 * 1024 * 1024


def _rms(x, g):
    return x * lax.rsqrt(jnp.mean(x * x, axis=-1, keepdims=True) + EPS) * g


def _s5_tables(lam_re, lam_im, b_re, b_im, c_re, c_im, log_dt):
    T, G, P, C = CHUNK, N_GROUPS, STATE_P, SSM_GROUP
    dt = jnp.exp(log_dt)[..., None]
    xr, xi = lam_re * dt, lam_im * dt
    n = jnp.arange(T + 1, dtype=F32)[:, None, None, None]
    mag = jnp.exp(n * xr)
    pr, pi = mag * jnp.cos(n * xi), mag * jnp.sin(n * xi)
    a_re, a_im = pr[1], pi[1]
    den = lam_re * lam_re + lam_im * lam_im
    co_re = ((a_re - 1.0) * lam_re + a_im * lam_im) / den
    co_im = (a_im * lam_re - (a_re - 1.0) * lam_im) / den
    bb_re = co_re[..., None] * b_re - co_im[..., None] * b_im
    bb_im = co_re[..., None] * b_im + co_im[..., None] * b_re
    e_re = pr[..., None] * bb_re - pi[..., None] * bb_im
    e_im = pr[..., None] * bb_im + pi[..., None] * bb_re
    f_re = c_re[None] * pr[:, :, :, None, :] - c_im[None] * pi[:, :, :, None, :]
    f_im = c_re[None] * pi[:, :, :, None, :] + c_im[None] * pr[:, :, :, None, :]
    kk = (jnp.einsum('dgcp,ndgpe->ndgce', c_re, e_re[:T], precision=HI)
          - jnp.einsum('dgcp,ndgpe->ndgce', c_im, e_im[:T], precision=HI))
    jj = np.arange(T)[:, None, None]
    ii = np.arange(T)[None, :, None]
    nn = np.arange(T)[None, None, :]
    sel = np.stack([(ii - jj == nn), (jj - ii == nn)], axis=0).astype(np.float32)
    toe = jnp.einsum('djin,ndgce->jigce', jnp.asarray(sel), kk, precision=HI)
    mt = jnp.transpose(toe, (2, 0, 4, 1, 3)).reshape(G, CW, CW)

    eye2 = jnp.eye(2, dtype=F32)
    si = jnp.stack([e_re[:T, 0][::-1], e_im[:T, 0][::-1], e_re[:T, 1], e_im[:T, 1]], axis=0)
    si = jnp.transpose(si, (2, 1, 4, 0, 3)).reshape(N_PAIRS, 2, CW, 4, 1, P)
    wsi = (si * eye2[None, :, None, None, :, None]).reshape(N_PAIRS, 2 * CW, 8 * P)

    so = jnp.stack([f_re[1:T + 1, 0], -f_im[1:T + 1, 0],
                    f_re[1:T + 1, 1][::-1], -f_im[1:T + 1, 1][::-1]], axis=0)
    so = jnp.transpose(so, (2, 0, 4, 1, 3)).reshape(N_PAIRS, 2, 4, P, 1, CW)
    so = jnp.transpose(so, (0, 2, 1, 3, 4, 5))
    wso = (so * eye2[None, None, :, None, :, None]).reshape(N_PAIRS, 8 * P, 2 * CW)

    at = jnp.stack([pr[T, 0], pi[T, 0], pr[T, 1], pi[T, 1]], axis=0)
    at = at.reshape(4, N_PAIRS, 2 * P).transpose(1, 0, 2).reshape(8, 8 * 2 * P)
    return mt.astype(BF16), wsi.astype(BF16), wso.astype(BF16), at


def _bias_table(rpb):
    qc = np.arange(GRID_W)[:, None]
    kc = np.arange(GRID_W)[None, :]
    q_start = np.clip(qc - WIN_W // 2, 0, GRID_W - WIN_W)
    valid = (kc >= q_start) & (kc < q_start + WIN_W)
    ci = np.clip(kc - qc, -(WIN_W - 1), WIN_W - 1) + (WIN_W - 1)
    onehot = (ci[None] == np.arange(2 * WIN_W - 1)[:, None, None]).astype(np.float32)
    t = jnp.einsum('hrc,cqk->hrqk', rpb.astype(F32), jnp.asarray(onehot), precision=HI)
    t = jnp.where(jnp.asarray(valid)[None, None], t, NEG)
    nrf = 2 * WIN_H - 2
    t = jnp.stack([t[:, :nrf], t[:, 1:nrf + 1]], axis=3)
    t = t.reshape(N_HEADS // 2, 2, nrf, GRID_W, 2 * GRID_W)
    return jnp.transpose(t, (0, 2, 1, 3, 4)).reshape(N_HEADS // 2, nrf, 2 * GRID_W, 2 * GRID_W)


def _in_proj_kernel(x_ref, g_ref, w_ref, o_ref):
    h = _rms(x_ref[...], g_ref[...]).astype(BF16)
    o_ref[...] = jnp.dot(h, w_ref[...], preferred_element_type=F32).astype(BF16)


def _state_in_kernel(u_ref, w_ref, s_ref):
    s_ref[...] = jnp.dot(u_ref[...], w_ref[0], preferred_element_type=F32)


def _scan_kernel(s_ref, a_ref, h_ref, *, cps):
    ar_f, ai_f = a_ref[:, 0:128], a_ref[:, 128:256]
    ar_b, ai_b = a_ref[:, 256:384], a_ref[:, 384:512]

    def body(k, carry):
        hfr, hfi, hbr, hbi = carry
        kb = cps - 1 - k
        h_ref[0, k, :, 0:128] = hfr
        h_ref[0, k, :, 128:256] = hfi
        h_ref[0, kb, :, 256:384] = hbr
        h_ref[0, kb, :, 384:512] = hbi
        nfr = ar_f * hfr - ai_f * hfi + s_ref[0, k, :, 0:128]
        nfi = ar_f * hfi + ai_f * hfr + s_ref[0, k, :, 128:256]
        nbr = ar_b * hbr - ai_b * hbi + s_ref[0, kb, :, 256:384]
        nbi = ar_b * hbi + ai_b * hbr + s_ref[0, kb, :, 384:512]
        return nfr, nfi, nbr, nbi

    z = jnp.zeros((8, 128), F32)
    lax.fori_loop(0, cps, body, (z, z, z, z))


def _s5_out_kernel(u_ref, h_ref, mt_ref, wso_ref, y_ref):
    u = u_ref[...]
    y = jnp.dot(h_ref[...].astype(BF16), wso_ref[0], preferred_element_type=F32)
    y0 = y[:, :CW] + jnp.dot(u[:, :CW], mt_ref[0], preferred_element_type=F32)
    y1 = y[:, CW:] + jnp.dot(u[:, CW:], mt_ref[1], preferred_element_type=F32)
    y_ref[:, :CW] = y0.astype(BF16)
    y_ref[:, CW:] = y1.astype(BF16)


def _attn_kernel(q_ref, k_ref, v_ref, b_ref, o_ref, *, rows):
    lane = lax.broadcasted_iota(jnp.int32, (GRID_W, 2 * HEAD_DIM), 1)
    first = lane < HEAD_DIM
    nkeys = WIN_H * GRID_W
    ones = jnp.ones((nkeys, 2 * HEAD_DIM), BF16)

    def one_row(r):
        rs = jnp.clip(r - WIN_H // 2, 0, rows - WIN_H)
        ri0 = rs - r + (WIN_H - 1)
        q0 = pl.multiple_of(r * GRID_W, GRID_W)
        k0 = pl.multiple_of(rs * GRID_W, GRID_W)
        q = q_ref[pl.ds(q0, GRID_W), :] * jnp.asarray(HEAD_DIM ** -0.5, BF16)
        zero = jnp.zeros_like(q)
        q2 = jnp.concatenate([jnp.where(first, q, zero), jnp.where(first, zero, q)], axis=0)
        kw = k_ref[pl.ds(k0, nkeys), :]
        vw = jnp.concatenate([v_ref[pl.ds(k0, nkeys), :], ones], axis=1)
        s = lax.dot_general(q2, kw, (((1,), (1,)), ((), ())), preferred_element_type=F32)
        s = s + jnp.concatenate([b_ref[0, ri0 + 2 * m] for m in range(WIN_H // 2)], axis=1)
        p = jnp.exp(s - jnp.max(s, axis=-1, keepdims=True))
        ol = jnp.dot(p.astype(BF16), vw, preferred_element_type=F32)
        o2 = ol[:, :2 * HEAD_DIM] / ol[:, 2 * HEAD_DIM:]
        o = jnp.where(first, o2[:GRID_W], o2[GRID_W:])
        o_ref[pl.ds(q0, GRID_W), :] = o.astype(BF16)

    def body(rb, carry):
        for i in range(ATTN_ROWS_PER_STEP):
            one_row(rb * ATTN_ROWS_PER_STEP + i)
        return carry

    lax.fori_loop(0, rows // ATTN_ROWS_PER_STEP, body, 0)


def _out_kernel(x_ref, y_ref, u_ref, zs_ref, o_ref, za_ref, dskip_ref, wglu_ref, bglu_ref,
                gs_ref, ga_ref, wout_ref, gfin_ref, out_ref):
    y = y_ref[...].astype(F32) + dskip_ref[...] * u_ref[...].astype(F32)
    y = jax.nn.gelu(y)
    gate = jnp.dot(y.astype(BF16), wglu_ref[...], preferred_element_type=F32) + bglu_ref[...]
    y = y * jax.nn.sigmoid(gate)
    ys = _rms(y, gs_ref[...]) * jax.nn.silu(zs_ref[...].astype(F32))
    ya = _rms(o_ref[...].astype(F32), ga_ref[...]) * jax.nn.silu(za_ref[...].astype(F32))
    mixed = jnp.concatenate([ys, ya], axis=-1).astype(BF16)
    out = x_ref[...] + jnp.dot(mixed, wout_ref[...], preferred_element_type=F32)
    out_ref[...] = _rms(out, gfin_ref[...])


def _params(**kw):
    return pltpu.CompilerParams(vmem_limit_bytes=VMEM_LIMIT, **kw)


def _trunk(x, tabs):
    (norm_g, w_in, mt, wsi, wso, at, bias, d_skip, w_glu, b_glu, gs, ga, w_out, gfin) = tabs
    bsz, seq, _ = x.shape
    n = bsz * seq
    nc = n // CHUNK
    cps = seq // CHUNK
    rows = seq // GRID_W
    tm = TOKEN_TILE
    x2 = x.reshape(n, D_MODEL)

    proj = pl.pallas_call(
        _in_proj_kernel,
        grid=(n // tm,),
        in_specs=[pl.BlockSpec((tm, D_MODEL), lambda i: (i, 0)),
                  pl.BlockSpec((1, D_MODEL), lambda i: (0, 0)),
                  pl.BlockSpec((D_MODEL, D_IN_PROJ), lambda i: (0, 0))],
        out_specs=pl.BlockSpec((tm, D_IN_PROJ), lambda i: (i, 0)),
        out_shape=jax.ShapeDtypeStruct((n, D_IN_PROJ), BF16),
        compiler_params=_params(),
        name="in_proj",
    )(x2, norm_g, w_in)

    u_c = proj[:, :D_SSM].reshape(nc, CHUNK, N_GROUPS, SSM_GROUP)
    u_c = jnp.transpose(u_c, (0, 2, 1, 3)).reshape(nc, N_GROUPS * CW)

    s_loc = pl.pallas_call(
        _state_in_kernel,
        grid=(N_PAIRS,),
        in_specs=[pl.BlockSpec((nc, 2 * CW), lambda p: (0, p)),
                  pl.BlockSpec((1, 2 * CW, 2 * CW), lambda p: (p, 0, 0))],
        out_specs=pl.BlockSpec((nc, 2 * CW), lambda p: (0, p)),
        out_shape=jax.ShapeDtypeStruct((nc, N_PAIRS * 2 * CW), F32),
        compiler_params=_params(),
        name="s5_state_in",
    )(u_c, wsi)

    h_in = pl.pallas_call(
        functools.partial(_scan_kernel, cps=cps),
        grid=(bsz, 2),
        in_specs=[pl.BlockSpec((1, cps, 8, 512), lambda b, m: (b, 0, 0, m)),
                  pl.BlockSpec((8, 512), lambda b, m: (0, m))],
        out_specs=pl.BlockSpec((1, cps, 8, 512), lambda b, m: (b, 0, 0, m)),
        out_shape=jax.ShapeDtypeStruct((bsz, cps, 8, 1024), F32),
        compiler_params=_params(),
        name="s5_scan",
    )(s_loc.reshape(bsz, cps, 8, 1024), at)

    y_c = pl.pallas_call(
        _s5_out_kernel,
        grid=(N_PAIRS,),
        in_specs=[pl.BlockSpec((nc, 2 * CW), lambda p: (0, p)),
                  pl.BlockSpec((nc, 2 * CW), lambda p: (0, p)),
                  pl.BlockSpec((2, CW, CW), lambda p: (p, 0, 0)),
                  pl.BlockSpec((1, 2 * CW, 2 * CW), lambda p: (p, 0, 0))],
        out_specs=pl.BlockSpec((nc, 2 * CW), lambda p: (0, p)),
        out_shape=jax.ShapeDtypeStruct((nc, N_PAIRS * 2 * CW), BF16),
        compiler_params=_params(),
        name="s5_out",
    )(u_c, h_in.reshape(nc, N_PAIRS * 2 * CW), mt, wso)

    y_s = jnp.transpose(y_c.reshape(nc, N_GROUPS, CHUNK, SSM_GROUP), (0, 2, 1, 3)).reshape(n, D_SSM)

    hp = 2 * HEAD_DIM
    o_attn = pl.pallas_call(
        functools.partial(_attn_kernel, rows=rows),
        grid=(N_HEADS // 2, bsz),
        in_specs=[pl.BlockSpec((seq, hp), lambda p, b: (b, 8 + p)),
                  pl.BlockSpec((seq, hp), lambda p, b: (b, 12 + p)),
                  pl.BlockSpec((seq, hp), lambda p, b: (b, 16 + p)),
                  pl.BlockSpec((1, 2 * WIN_H - 2, 2 * GRID_W, 2 * GRID_W), lambda p, b: (p, 0, 0, 0))],
        out_specs=pl.BlockSpec((seq, hp), lambda p, b: (b, p)),
        out_shape=jax.ShapeDtypeStruct((n, D_ATTN), BF16),
        compiler_params=_params(),
        name="attention",
    )(proj, proj, proj, bias)

    vec = lambda width: pl.BlockSpec((1, width), lambda i: (0, 0))
    out = pl.pallas_call(
        _out_kernel,
        grid=(n // tm,),
        in_specs=[pl.BlockSpec((tm, D_MODEL), lambda i: (i, 0)),
                  pl.BlockSpec((tm, D_SSM), lambda i: (i, 0)),
                  pl.BlockSpec((tm, D_SSM), lambda i: (i, 0)),
                  pl.BlockSpec((tm, D_SSM), lambda i: (i, 1)),
                  pl.BlockSpec((tm, D_ATTN), lambda i: (i, 0)),
                  pl.BlockSpec((tm, D_ATTN), lambda i: (i, 5)),
                  vec(D_SSM),
                  pl.BlockSpec((D_SSM, D_SSM), lambda i: (0, 0)),
                  vec(D_SSM), vec(D_SSM), vec(D_ATTN),
                  pl.BlockSpec((D_MODEL, D_MODEL), lambda i: (0, 0)),
                  vec(D_MODEL)],
        out_specs=pl.BlockSpec((tm, D_MODEL), lambda i: (i, 0)),
        out_shape=jax.ShapeDtypeStruct((n, D_MODEL), F32),
        compiler_params=_params(),
        name="out_proj",
    )(x2, y_s, proj, proj, o_attn, proj, d_skip, w_glu, b_glu, gs, ga, w_out, gfin)
    return out.reshape(bsz, seq, D_MODEL)


def kernel(x_prompt, x_sample, norm_g, w_in, lam_re, lam_im, b_re, b_im, c_re, c_im, log_dt,
           d_skip, w_glu, b_glu, rpb, ssm_out_g, attn_out_g, w_out, final_norm_g):
    assert norm_g.shape[0] == 1, "single layer only"
    mt, wsi, wso, at = _s5_tables(lam_re[0], lam_im[0], b_re[0], b_im[0], c_re[0], c_im[0], log_dt[0])
    tabs = (norm_g[0][None], w_in[0].astype(BF16), mt, wsi, wso, at, _bias_table(rpb[0]),
            d_skip[0][None], w_glu[0].astype(BF16), b_glu[0][None], ssm_out_g[0][None],
            attn_out_g[0][None], w_out[0].astype(BF16), final_norm_g[None])
    return _trunk(x_prompt, tabs), _trunk(x_sample, tabs)
```

```python
import functools

import jax
import jax.numpy as jnp
import numpy as np
from jax import lax
from jax.experimental import pallas as pl
from jax.experimental.pallas import tpu as pltpu

F32 = jnp.float32
BF16 = jnp.bfloat16
HI = lax.Precision.HIGHEST

D_MODEL = 1024
D_SSM = 512
SSM_GROUP = 16
N_GROUPS = D_SSM // SSM_GROUP
N_PAIRS = N_GROUPS // 2
STATE_P = 64
N_HEADS = 8
HEAD_DIM = 64
D_ATTN = N_HEADS * HEAD_DIM
D_NAT = 3 * D_ATTN + D_ATTN
D_CM = 2 * D_SSM
GRID_W = 64
WIN_H = 8
WIN_W = 16
EPS = 1e-6
CHUNK = 16
PW = 2 * CHUNK * SSM_GROUP
NEG = -1e30

CHUNK_TILE = 512
ATTN_ROWS_PER_STEP = 4
LANES = 128
VMEM_LIMIT = 56 * 1024 * 1024


def _rms(x, g):
    return x * lax.rsqrt(jnp.mean(x * x, axis=-1, keepdims=True) + EPS) * g


def _s5_tables(lam_re, lam_im, b_re, b_im, c_re, c_im, log_dt):
    T, G, P, C = CHUNK, N_GROUPS, STATE_P, SSM_GROUP
    dt = jnp.exp(log_dt)[..., None]
    xr, xi = lam_re * dt, lam_im * dt
    n = jnp.arange(T + 1, dtype=F32)[:, None, None, None]
    mag = jnp.exp(n * xr)
    pr, pi = mag * jnp.cos(n * xi), mag * jnp.sin(n * xi)
    a_re, a_im = pr[1], pi[1]
    den = lam_re * lam_re + lam_im * lam_im
    co_re = ((a_re - 1.0) * lam_re + a_im * lam_im) / den
    co_im = (a_im * lam_re - (a_re - 1.0) * lam_im) / den
    bb_re = co_re[..., None] * b_re - co_im[..., None] * b_im
    bb_im = co_re[..., None] * b_im + co_im[..., None] * b_re
    e_re = pr[..., None] * bb_re - pi[..., None] * bb_im
    e_im = pr[..., None] * bb_im + pi[..., None] * bb_re
    f_re = c_re[None] * pr[:, :, :, None, :] - c_im[None] * pi[:, :, :, None, :]
    f_im = c_re[None] * pi[:, :, :, None, :] + c_im[None] * pr[:, :, :, None, :]
    kk = (jnp.einsum('dgcp,ndgpe->ndgce', c_re, e_re[:T], precision=HI)
          - jnp.einsum('dgcp,ndgpe->ndgce', c_im, e_im[:T], precision=HI))
    jj = np.arange(T)[:, None, None]
    ii = np.arange(T)[None, :, None]
    nn = np.arange(T)[None, None, :]
    sel = np.stack([(ii - jj == nn), (jj - ii == nn)], axis=0).astype(np.float32)
    toe = jnp.einsum('djin,ndgce->jigce', jnp.asarray(sel), kk, precision=HI)

    eye2 = jnp.eye(2, dtype=F32)[None, None, :, None, None, :, None]
    mt = jnp.transpose(toe.reshape(T, T, N_PAIRS, 2, C, C), (2, 1, 3, 4, 0, 5))
    mt = (mt[:, :, :, :, :, None, :] * eye2).reshape(N_PAIRS, PW, PW)
    si = jnp.stack([e_re[:T, 0][::-1], e_im[:T, 0][::-1], e_re[:T, 1], e_im[:T, 1]], axis=0)
    si = jnp.transpose(si.reshape(4, T, N_PAIRS, 2, P, C), (2, 1, 3, 5, 0, 4))
    wsi = (si[:, :, :, :, :, None, :] * eye2).reshape(N_PAIRS, PW, 8 * P)
    so = jnp.stack([f_re[1:T + 1, 0], -f_im[1:T + 1, 0],
                    f_re[1:T + 1, 1][::-1], -f_im[1:T + 1, 1][::-1]], axis=0)
    so = jnp.transpose(so.reshape(4, T, N_PAIRS, 2, C, P), (2, 1, 3, 4, 0, 5))
    wso = (so[:, :, :, :, :, None, :] * eye2).reshape(N_PAIRS, PW, 8 * P)

    at = jnp.stack([pr[T, 0], pi[T, 0], pr[T, 1], pi[T, 1]], axis=0)
    at = at.reshape(4, N_PAIRS, 2 * P).transpose(1, 0, 2).reshape(8, 8 * 2 * P)
    return mt.astype(BF16), wsi.astype(BF16), wso.astype(BF16), at


def _bias_table(rpb):
    qc = np.arange(GRID_W)[:, None]
    kc = np.arange(GRID_W)[None, :]
    q_start = np.clip(qc - WIN_W // 2, 0, GRID_W - WIN_W)
    valid = (kc >= q_start) & (kc < q_start + WIN_W)
    ci = np.clip(kc - qc, -(WIN_W - 1), WIN_W - 1) + (WIN_W - 1)
    onehot = (ci[None] == np.arange(2 * WIN_W - 1)[:, None, None]).astype(np.float32)
    t = jnp.einsum('hrc,cqk->hrqk', rpb.astype(F32), jnp.asarray(onehot), precision=HI)
    t = jnp.where(jnp.asarray(valid)[None, None], t, NEG)
    nrf = 2 * WIN_H - 2
    t = jnp.stack([t[:, :nrf], t[:, 1:nrf + 1]], axis=3)
    t = t.reshape(N_HEADS // 2, 2, nrf, GRID_W, 2 * GRID_W)
    return jnp.transpose(t, (0, 2, 1, 3, 4)).reshape(N_HEADS // 2, nrf, 2 * GRID_W, 2 * GRID_W)


def _col(v):
    return jnp.broadcast_to(v.astype(F32)[:, None], (v.shape[0], LANES))


def _in_proj_kernel(x_ref, g_ref, wn_ref, wc_ref, nat_ref, cm_ref):
    h = _rms(x_ref[...], g_ref[...]).astype(BF16)
    nat_ref[...] = jnp.dot(h, wn_ref[...], preferred_element_type=F32).astype(BF16)
    cm = lax.dot_general(wc_ref[...], h, (((1,), (1,)), ((), ())), preferred_element_type=F32)
    cm_ref[...] = cm.astype(BF16)


def _state_in_kernel(u_ref, w_ref, s_ref):
    z = u_ref[...].reshape(PW, u_ref.shape[-1])
    s_ref[...] = lax.dot_general(z, w_ref[0], (((0,), (0,)), ((), ())), preferred_element_type=F32)


def _scan_kernel(s_ref, a_ref, h_ref, *, cps):
    ar_f, ai_f = a_ref[:, 0:128], a_ref[:, 128:256]
    ar_b, ai_b = a_ref[:, 256:384], a_ref[:, 384:512]

    def body(k, carry):
        hfr, hfi, hbr, hbi = carry
        kb = cps - 1 - k
        h_ref[0, k, :, 0:128] = hfr
        h_ref[0, k, :, 128:256] = hfi
        h_ref[0, kb, :, 256:384] = hbr
        h_ref[0, kb, :, 384:512] = hbi
        nfr = ar_f * hfr - ai_f * hfi + s_ref[0, k, :, 0:128]
        nfi = ar_f * hfi + ai_f * hfr + s_ref[0, k, :, 128:256]
        nbr = ar_b * hbr - ai_b * hbi + s_ref[0, kb, :, 256:384]
        nbi = ar_b * hbi + ai_b * hbr + s_ref[0, kb, :, 384:512]
        return nfr, nfi, nbr, nbi

    z = jnp.zeros((8, 128), F32)
    lax.fori_loop(0, cps, body, (z, z, z, z))


def _s5_out_kernel(u_ref, h_ref, mt_ref, wso_ref, y_ref):
    nc = u_ref.shape[-1]
    z = u_ref[...].reshape(PW, nc)
    y = jnp.dot(mt_ref[0], z, preferred_element_type=F32)
    y = y + lax.dot_general(wso_ref[0], h_ref[...].astype(BF16), (((1,), (1,)), ((), ())),
                            preferred_element_type=F32)
    y_ref[...] = y.astype(BF16).reshape(CHUNK, 2 * SSM_GROUP, nc)


def _attn_kernel(q_ref, k_ref, v_ref, b_ref, o_ref, *, rows):
    lane = lax.broadcasted_iota(jnp.int32, (GRID_W, 2 * HEAD_DIM), 1)
    first = lane < HEAD_DIM
    nkeys = WIN_H * GRID_W
    ones = jnp.ones((nkeys, 2 * HEAD_DIM), BF16)

    def one_row(r):
        rs = jnp.clip(r - WIN_H // 2, 0, rows - WIN_H)
        ri0 = rs - r + (WIN_H - 1)
        q0 = pl.multiple_of(r * GRID_W, GRID_W)
        k0 = pl.multiple_of(rs * GRID_W, GRID_W)
        q = q_ref[pl.ds(q0, GRID_W), :] * jnp.asarray(HEAD_DIM ** -0.5, BF16)
        zero = jnp.zeros_like(q)
        q2 = jnp.concatenate([jnp.where(first, q, zero), jnp.where(first, zero, q)], axis=0)
        kw = k_ref[pl.ds(k0, nkeys), :]
        vw = jnp.concatenate([v_ref[pl.ds(k0, nkeys), :], ones], axis=1)
        s = lax.dot_general(q2, kw, (((1,), (1,)), ((), ())), preferred_element_type=F32)
        s = s + jnp.concatenate([b_ref[0, ri0 + 2 * m] for m in range(WIN_H // 2)], axis=1)
        p = jnp.exp(s - jnp.max(s, axis=-1, keepdims=True))
        ol = jnp.dot(p.astype(BF16), vw, preferred_element_type=F32)
        o2 = ol[:, :2 * HEAD_DIM] / ol[:, 2 * HEAD_DIM:]
        o = jnp.where(first, o2[:GRID_W], o2[GRID_W:])
        o_ref[pl.ds(q0, GRID_W), :] = o.astype(BF16)

    def body(rb, carry):
        for i in range(ATTN_ROWS_PER_STEP):
            one_row(rb * ATTN_ROWS_PER_STEP + i)
        return carry

    lax.fori_loop(0, rows // ATTN_ROWS_PER_STEP, body, 0)


def _out_kernel(x_ref, y_ref, uz_ref, o_ref, za_ref, dskip_ref, wglu_ref, bglu_ref,
                gs_ref, ga_ref, wout_ref, gfin_ref, out_ref):
    kt = y_ref.shape[-1]
    lanes = lambda r: jnp.concatenate([r[...]] * (kt // LANES), axis=1)
    u = uz_ref[:D_SSM, :].astype(F32)
    zs = uz_ref[D_SSM:, :].astype(F32)
    y = jax.nn.gelu(y_ref[...].astype(F32) + lanes(dskip_ref) * u)
    gate = jnp.dot(wglu_ref[...], y.astype(BF16), preferred_element_type=F32) + lanes(bglu_ref)
    y = y * jax.nn.sigmoid(gate)
    y = y * lax.rsqrt(jnp.mean(y * y, axis=0, keepdims=True) + EPS) * lanes(gs_ref)
    ys = (y * jax.nn.silu(zs)).astype(BF16).T
    ya = _rms(o_ref[...].astype(F32), ga_ref[...]) * jax.nn.silu(za_ref[...].astype(F32))
    mixed = jnp.concatenate([ys, ya.astype(BF16)], axis=-1)
    out = x_ref[...] + jnp.dot(mixed, wout_ref[...], preferred_element_type=F32)
    out_ref[...] = _rms(out, gfin_ref[...])


def _params(**kw):
    return pltpu.CompilerParams(vmem_limit_bytes=VMEM_LIMIT, **kw)


def _trunk(x, tabs):
    (norm_g, w_nat, w_cm, mt, wsi, wso, at, bias, d_skip, w_glu_t, b_glu, gs, ga, w_out, gfin) = tabs
    bsz, seq, _ = x.shape
    n = bsz * seq
    nc = n // CHUNK
    cps = seq // CHUNK
    rows = seq // GRID_W
    kt = CHUNK_TILE
    x3 = x.reshape(nc, CHUNK * D_MODEL)

    nat, cm = pl.pallas_call(
        _in_proj_kernel,
        grid=(nc // kt, CHUNK),
        in_specs=[pl.BlockSpec((kt, D_MODEL), lambda t, j: (t, j)),
                  pl.BlockSpec((1, D_MODEL), lambda t, j: (0, 0)),
                  pl.BlockSpec((D_MODEL, D_NAT), lambda t, j: (0, 0)),
                  pl.BlockSpec((D_CM, D_MODEL), lambda t, j: (0, 0))],
        out_specs=[pl.BlockSpec((kt, D_NAT), lambda t, j: (t, j)),
                   pl.BlockSpec((None, D_CM, kt), lambda t, j: (j, 0, t))],
        out_shape=[jax.ShapeDtypeStruct((nc, CHUNK * D_NAT), BF16),
                   jax.ShapeDtypeStruct((CHUNK, D_CM, nc), BF16)],
        compiler_params=_params(),
        name="in_proj",
    )(x3, norm_g, w_nat, w_cm)

    pair_rows = 2 * SSM_GROUP
    s_loc = pl.pallas_call(
        _state_in_kernel,
        grid=(N_PAIRS,),
        in_specs=[pl.BlockSpec((CHUNK, pair_rows, nc), lambda p: (0, p, 0)),
                  pl.BlockSpec((1, PW, PW), lambda p: (p, 0, 0))],
        out_specs=pl.BlockSpec((nc, PW), lambda p: (0, p)),
        out_shape=jax.ShapeDtypeStruct((nc, N_PAIRS * PW), F32),
        compiler_params=_params(),
        name="s5_state_in",
    )(cm, wsi)

    h_in = pl.pallas_call(
        functools.partial(_scan_kernel, cps=cps),
        grid=(bsz, 2),
        in_specs=[pl.BlockSpec((1, cps, 8, 512), lambda b, m: (b, 0, 0, m)),
                  pl.BlockSpec((8, 512), lambda b, m: (0, m))],
        out_specs=pl.BlockSpec((1, cps, 8, 512), lambda b, m: (b, 0, 0, m)),
        out_shape=jax.ShapeDtypeStruct((bsz, cps, 8, 1024), F32),
        compiler_params=_params(),
        name="s5_scan",
    )(s_loc.reshape(bsz, cps, 8, 1024), at)

    y_cm = pl.pallas_call(
        _s5_out_kernel,
        grid=(N_PAIRS,),
        in_specs=[pl.BlockSpec((CHUNK, pair_rows, nc), lambda p: (0, p, 0)),
                  pl.BlockSpec((nc, PW), lambda p: (0, p)),
                  pl.BlockSpec((1, PW, PW), lambda p: (p, 0, 0)),
                  pl.BlockSpec((1, PW, PW), lambda p: (p, 0, 0))],
        out_specs=pl.BlockSpec((CHUNK, pair_rows, nc), lambda p: (0, p, 0)),
        out_shape=jax.ShapeDtypeStruct((CHUNK, D_SSM, nc), BF16),
        compiler_params=_params(),
        name="s5_out",
    )(cm, h_in.reshape(nc, N_PAIRS * PW), mt, wso)

    nat2 = nat.reshape(n, D_NAT)
    hp = 2 * HEAD_DIM
    o_attn = pl.pallas_call(
        functools.partial(_attn_kernel, rows=rows),
        grid=(N_HEADS // 2, bsz),
        in_specs=[pl.BlockSpec((seq, hp), lambda p, b: (b, p)),
                  pl.BlockSpec((seq, hp), lambda p, b: (b, 4 + p)),
                  pl.BlockSpec((seq, hp), lambda p, b: (b, 8 + p)),
                  pl.BlockSpec((1, 2 * WIN_H - 2, 2 * GRID_W, 2 * GRID_W), lambda p, b: (p, 0, 0, 0))],
        out_specs=pl.BlockSpec((seq, hp), lambda p, b: (b, p)),
        out_shape=jax.ShapeDtypeStruct((n, D_ATTN), BF16),
        compiler_params=_params(),
        name="attention",
    )(nat2, nat2, nat2, bias)

    colv = lambda width: pl.BlockSpec((width, LANES), lambda t, i: (0, 0))
    rowv = lambda width: pl.BlockSpec((1, width), lambda t, i: (0, 0))
    out = pl.pallas_call(
        _out_kernel,
        grid=(nc // kt, CHUNK),
        in_specs=[pl.BlockSpec((kt, D_MODEL), lambda t, i: (t, i)),
                  pl.BlockSpec((None, D_SSM, kt), lambda t, i: (i, 0, t)),
                  pl.BlockSpec((None, D_CM, kt), lambda t, i: (i, 0, t)),
                  pl.BlockSpec((kt, D_ATTN), lambda t, i: (t, i)),
                  pl.BlockSpec((kt, D_ATTN), lambda t, i: (t, 4 * i + 3)),
                  colv(D_SSM),
                  pl.BlockSpec((D_SSM, D_SSM), lambda t, i: (0, 0)),
                  colv(D_SSM), colv(D_SSM), rowv(D_ATTN),
                  pl.BlockSpec((D_MODEL, D_MODEL), lambda t, i: (0, 0)),
                  rowv(D_MODEL)],
        out_specs=pl.BlockSpec((kt, D_MODEL), lambda t, i: (t, i)),
        out_shape=jax.ShapeDtypeStruct((nc, CHUNK * D_MODEL), F32),
        compiler_params=_params(),
        name="out_proj",
    )(x3, y_cm, cm, o_attn.reshape(nc, CHUNK * D_ATTN), nat, d_skip, w_glu_t, b_glu, gs, ga, w_out, gfin)
    return out.reshape(bsz, seq, D_MODEL)


def kernel(x_prompt, x_sample, norm_g, w_in, lam_re, lam_im, b_re, b_im, c_re, c_im, log_dt,
           d_skip, w_glu, b_glu, rpb, ssm_out_g, attn_out_g, w_out, final_norm_g):
    assert norm_g.shape[0] == 1, "single layer only"
    mt, wsi, wso, at = _s5_tables(lam_re[0], lam_im[0], b_re[0], b_im[0], c_re[0], c_im[0], log_dt[0])
    w = w_in[0].astype(BF16)
    w_cm = w[:, :D_CM].T
    w_nat = w[:, D_CM:]
    tabs = (norm_g[0][None], w_nat, w_cm, mt, wsi, wso, at, _bias_table(rpb[0]),
            _col(d_skip[0]), w_glu[0].astype(BF16).T, _col(b_glu[0]), _col(ssm_out_g[0]),
            attn_out_g[0][None], w_out[0].astype(BF16), final_norm_g[None])
    return _trunk(x_prompt, tabs), _trunk(x_sample, tabs)
```

```python
import functools

import jax
import jax.numpy as jnp
import numpy as np
from jax import lax
from jax.experimental import pallas as pl
from jax.experimental.pallas import tpu as pltpu

F32 = jnp.float32
BF16 = jnp.bfloat16
HI = lax.Precision.HIGHEST

D_MODEL = 1024
D_SSM = 512
SSM_GROUP = 16
N_GROUPS = D_SSM // SSM_GROUP
N_PAIRS = N_GROUPS // 2
STATE_P = 64
N_HEADS = 8
HEAD_DIM = 64
D_ATTN = N_HEADS * HEAD_DIM
D_NAT = 3 * D_ATTN + D_ATTN
D_CM = 2 * D_SSM
GRID_W = 64
WIN_H = 8
WIN_W = 16
EPS = 1e-6
CHUNK = 16
PW = 2 * CHUNK * SSM_GROUP
NEG = -1e30

LANES = 128
TILE_CHUNKS = LANES
TOKEN_TILE = TILE_CHUNKS * CHUNK
SUB_STEPS = 4
SUB_TILE = TOKEN_TILE // SUB_STEPS
POS_PER_STEP = CHUNK // SUB_STEPS
N_SLABS = D_MODEL // LANES
SCAN_PAIRS = 8
ATTN_ROWS_PER_STEP = 4
VMEM_LIMIT = 56 * 1024 * 1024


def _rms(x, g):
    return x * lax.rsqrt(jnp.mean(x * x, axis=-1, keepdims=True) + EPS) * g


def _s5_tables(lam_re, lam_im, b_re, b_im, c_re, c_im, log_dt):
    T, G, P, C = CHUNK, N_GROUPS, STATE_P, SSM_GROUP
    dt = jnp.exp(log_dt)[..., None]
    xr, xi = lam_re * dt, lam_im * dt
    n = jnp.arange(T + 1, dtype=F32)[:, None, None, None]
    mag = jnp.exp(n * xr)
    pr, pi = mag * jnp.cos(n * xi), mag * jnp.sin(n * xi)
    a_re, a_im = pr[1], pi[1]
    den = lam_re * lam_re + lam_im * lam_im
    co_re = ((a_re - 1.0) * lam_re + a_im * lam_im) / den
    co_im = (a_im * lam_re - (a_re - 1.0) * lam_im) / den
    bb_re = co_re[..., None] * b_re - co_im[..., None] * b_im
    bb_im = co_re[..., None] * b_im + co_im[..., None] * b_re
    e_re = pr[..., None] * bb_re - pi[..., None] * bb_im
    e_im = pr[..., None] * bb_im + pi[..., None] * bb_re
    f_re = c_re[None] * pr[:, :, :, None, :] - c_im[None] * pi[:, :, :, None, :]
    f_im = c_re[None] * pi[:, :, :, None, :] + c_im[None] * pr[:, :, :, None, :]
    kk = (jnp.einsum('dgcp,ndgpe->ndgce', c_re, e_re[:T], precision=HI)
          - jnp.einsum('dgcp,ndgpe->ndgce', c_im, e_im[:T], precision=HI))
    jj = np.arange(T)[:, None, None]
    ii = np.arange(T)[None, :, None]
    nn = np.arange(T)[None, None, :]
    sel = np.stack([(ii - jj == nn), (jj - ii == nn)], axis=0).astype(np.float32)
    toe = jnp.einsum('djin,ndgce->jigce', jnp.asarray(sel), kk, precision=HI)

    eye2 = jnp.eye(2, dtype=F32)[None, None, :, None, None, :, None]
    mt = jnp.transpose(toe.reshape(T, T, N_PAIRS, 2, C, C), (2, 1, 3, 4, 0, 5))
    mt = (mt[:, :, :, :, :, None, :] * eye2).reshape(N_PAIRS, PW, PW)
    si = jnp.stack([e_re[:T, 0][::-1], e_im[:T, 0][::-1], e_re[:T, 1], e_im[:T, 1]], axis=0)
    si = jnp.transpose(si.reshape(4, T, N_PAIRS, 2, P, C), (2, 1, 3, 5, 0, 4))
    wsi = (si[:, :, :, :, :, None, :] * eye2).reshape(N_PAIRS, PW, 8 * P)
    so = jnp.stack([f_re[1:T + 1, 0], -f_im[1:T + 1, 0],
                    f_re[1:T + 1, 1][::-1], -f_im[1:T + 1, 1][::-1]], axis=0)
    so = jnp.transpose(so.reshape(4, T, N_PAIRS, 2, C, P), (2, 1, 3, 4, 0, 5))
    wso = (so[:, :, :, :, :, None, :] * eye2).reshape(N_PAIRS, PW, 8 * P)

    at = jnp.stack([pr[T, 0], pi[T, 0], pr[T, 1], pi[T, 1]], axis=0)
    at = at.reshape(4, N_PAIRS, 2 * P).transpose(1, 0, 2).reshape(N_PAIRS, 8 * P)
    return mt.astype(BF16), wsi.astype(BF16), wso.astype(BF16), at


def _bias_table(rpb):
    qc = np.arange(GRID_W)[:, None]
    kc = np.arange(GRID_W)[None, :]
    q_start = np.clip(qc - WIN_W // 2, 0, GRID_W - WIN_W)
    valid = (kc >= q_start) & (kc < q_start + WIN_W)
    ci = np.clip(kc - qc, -(WIN_W - 1), WIN_W - 1) + (WIN_W - 1)
    onehot = (ci[None] == np.arange(2 * WIN_W - 1)[:, None, None]).astype(np.float32)
    t = jnp.einsum('hrc,cqk->hrqk', rpb.astype(F32), jnp.asarray(onehot), precision=HI)
    t = jnp.where(jnp.asarray(valid)[None, None], t, NEG)
    nrf = 2 * WIN_H - 2
    t = jnp.stack([t[:, :nrf], t[:, 1:nrf + 1]], axis=3)
    t = t.reshape(N_HEADS // 2, 2, nrf, GRID_W, 2 * GRID_W)
    return jnp.transpose(t, (0, 2, 1, 3, 4)).reshape(N_HEADS // 2, nrf, 2 * GRID_W, 2 * GRID_W)


def _col(v):
    return jnp.broadcast_to(v.astype(F32)[:, None], (v.shape[0], LANES))


def _in_proj_kernel(*refs):
    xs = refs[:N_SLABS]
    g_ref, wn_ref, wc_ref, nat_ref, cm_ref = refs[N_SLABS:]
    s = pl.program_id(1)
    g = g_ref[...]
    r0 = pl.multiple_of(s * SUB_TILE, SUB_TILE)
    xn = jnp.concatenate([x[pl.ds(r0, SUB_TILE), :] for x in xs], axis=1)
    h = _rms(xn, g).astype(BF16)
    nat_ref[...] = jnp.dot(h, wn_ref[...], preferred_element_type=F32).astype(BF16)
    hp = []
    for jl in range(POS_PER_STEP):
        j = s * POS_PER_STEP + jl
        xj = jnp.concatenate([x[pl.ds(j, TILE_CHUNKS, stride=CHUNK), :] for x in xs], axis=1)
        hp.append(_rms(xj, g).astype(BF16))
    hp = jnp.concatenate(hp, axis=0)
    cm = lax.dot_general(wc_ref[...], hp, (((1,), (1,)), ((), ())), preferred_element_type=F32)
    for jl in range(POS_PER_STEP):
        cm_ref[jl] = cm[:, jl * TILE_CHUNKS:(jl + 1) * TILE_CHUNKS].astype(BF16)


def _state_in_kernel(u_ref, w_ref, s_ref):
    z = u_ref[...].reshape(PW, u_ref.shape[-1])
    s_ref[...] = lax.dot_general(z, w_ref[0], (((0,), (0,)), ((), ())), preferred_element_type=F32)


def _scan_kernel(s_ref, a_ref, h_ref, sin, sout, *, cps):
    pitch = cps + 8
    for pr in range(SCAN_PAIRS):
        for part in range(4):
            slab = part * SCAN_PAIRS + pr
            lane0 = (pr * 4 + part) * LANES
            sin[pl.ds(slab * pitch, cps), :] = s_ref[:, lane0:lane0 + LANES]

    ar_f, ai_f = a_ref[:, 0:128], a_ref[:, 128:256]
    ar_b, ai_b = a_ref[:, 256:384], a_ref[:, 384:512]
    rows = lambda part, k: pl.ds(part * SCAN_PAIRS * pitch + k, SCAN_PAIRS, stride=pitch)

    def body(k, carry):
        hfr, hfi, hbr, hbi = carry
        kb = cps - 1 - k
        sout[rows(0, k), :] = hfr
        sout[rows(1, k), :] = hfi
        sout[rows(2, kb), :] = hbr
        sout[rows(3, kb), :] = hbi
        nfr = ar_f * hfr - ai_f * hfi + sin[rows(0, k), :]
        nfi = ar_f * hfi + ai_f * hfr + sin[rows(1, k), :]
        nbr = ar_b * hbr - ai_b * hbi + sin[rows(2, kb), :]
        nbi = ar_b * hbi + ai_b * hbr + sin[rows(3, kb), :]
        return nfr, nfi, nbr, nbi

    z = jnp.zeros((SCAN_PAIRS, LANES), F32)
    lax.fori_loop(0, cps, body, (z, z, z, z))

    for pr in range(SCAN_PAIRS):
        for part in range(4):
            slab = part * SCAN_PAIRS + pr
            lane0 = (pr * 4 + part) * LANES
            h_ref[:, lane0:lane0 + LANES] = sout[pl.ds(slab * pitch, cps), :]


def _s5_out_kernel(u_ref, h_ref, mt_ref, wso_ref, y_ref):
    nc = u_ref.shape[-1]
    z = u_ref[...].reshape(PW, nc)
    y = jnp.dot(mt_ref[0], z, preferred_element_type=F32)
    y = y + lax.dot_general(wso_ref[0], h_ref[...].astype(BF16), (((1,), (1,)), ((), ())),
                            preferred_element_type=F32)
    y_ref[...] = y.astype(BF16).reshape(CHUNK, 2 * SSM_GROUP, nc)


def _attn_kernel(q_ref, k_ref, v_ref, b_ref, o_ref, *, rows):
    lane = lax.broadcasted_iota(jnp.int32, (GRID_W, 2 * HEAD_DIM), 1)
    first = lane < HEAD_DIM
    nkeys = WIN_H * GRID_W
    ones = jnp.ones((nkeys, 2 * HEAD_DIM), BF16)

    def one_row(r):
        rs = jnp.clip(r - WIN_H // 2, 0, rows - WIN_H)
        ri0 = rs - r + (WIN_H - 1)
        q0 = pl.multiple_of(r * GRID_W, GRID_W)
        k0 = pl.multiple_of(rs * GRID_W, GRID_W)
        q = q_ref[pl.ds(q0, GRID_W), :] * jnp.asarray(HEAD_DIM ** -0.5, BF16)
        zero = jnp.zeros_like(q)
        q2 = jnp.concatenate([jnp.where(first, q, zero), jnp.where(first, zero, q)], axis=0)
        kw = k_ref[pl.ds(k0, nkeys), :]
        vw = jnp.concatenate([v_ref[pl.ds(k0, nkeys), :], ones], axis=1)
        s = lax.dot_general(q2, kw, (((1,), (1,)), ((), ())), preferred_element_type=F32)
        s = s + jnp.concatenate([b_ref[0, ri0 + 2 * m] for m in range(WIN_H // 2)], axis=1)
        p = jnp.exp(s - jnp.max(s, axis=-1, keepdims=True))
        ol = jnp.dot(p.astype(BF16), vw, preferred_element_type=F32)
        o2 = ol[:, :2 * HEAD_DIM] / ol[:, 2 * HEAD_DIM:]
        o = jnp.where(first, o2[:GRID_W], o2[GRID_W:])
        o_ref[pl.ds(q0, GRID_W), :] = o.astype(BF16)

    def body(rb, carry):
        for i in range(ATTN_ROWS_PER_STEP):
            one_row(rb * ATTN_ROWS_PER_STEP + i)
        return carry

    lax.fori_loop(0, rows // ATTN_ROWS_PER_STEP, body, 0)


def _out_kernel(x_ref, y_ref, uz_ref, o_ref, za_ref, dskip_ref, wglu_ref, bglu_ref,
                gs_ref, ga_ref, wout_ref, gfin_ref, out_ref, ys_scr):
    s = pl.program_id(1)
    two = lambda r: jnp.concatenate([r[...], r[...]], axis=1)

    @pl.when(s == 0)
    def _():
        for i0 in range(0, CHUNK, 2):
            cat = lambda ref, lo, hi: jnp.concatenate(
                [ref[i0, lo:hi, :], ref[i0 + 1, lo:hi, :]], axis=1).astype(F32)
            y = jax.nn.gelu(cat(y_ref, 0, D_SSM) + two(dskip_ref) * cat(uz_ref, 0, D_SSM))
            gate = jnp.dot(wglu_ref[...], y.astype(BF16), preferred_element_type=F32) + two(bglu_ref)
            y = y * jax.nn.sigmoid(gate)
            y = y * lax.rsqrt(jnp.mean(y * y, axis=0, keepdims=True) + EPS) * two(gs_ref)
            y = y * jax.nn.silu(cat(uz_ref, D_SSM, D_CM))
            for d in range(2):
                yt = y[:, d * LANES:(d + 1) * LANES].T
                for sl in range(D_SSM // LANES):
                    ys_scr[sl, pl.ds(i0 + d, TILE_CHUNKS, stride=CHUNK), :] = yt[:, sl * LANES:(sl + 1) * LANES]

    r0 = pl.multiple_of(s * SUB_TILE, SUB_TILE)
    ys = jnp.concatenate([ys_scr[sl, pl.ds(r0, SUB_TILE), :] for sl in range(D_SSM // LANES)], axis=1)
    ya = _rms(o_ref[...].astype(F32), ga_ref[...]) * jax.nn.silu(za_ref[...].astype(F32))
    mixed = jnp.concatenate([ys.astype(BF16), ya.astype(BF16)], axis=-1)
    out = x_ref[...] + jnp.dot(mixed, wout_ref[...], preferred_element_type=F32)
    out_ref[...] = _rms(out, gfin_ref[...])


def _params(**kw):
    return pltpu.CompilerParams(vmem_limit_bytes=VMEM_LIMIT, **kw)


def _trunk(x, tabs):
    (norm_g, w_nat, w_cm, mt, wsi, wso, at, bias, d_skip, w_glu_t, b_glu, gs, ga, w_out, gfin) = tabs
    bsz, seq, _ = x.shape
    n = bsz * seq
    nc = n // CHUNK
    cps = seq // CHUNK
    rows = seq // GRID_W
    x2 = x.reshape(n, D_MODEL)
    n_tiles = n // TOKEN_TILE
    sub = lambda t, s: (t * SUB_STEPS + s, 0)

    nat, cm = pl.pallas_call(
        _in_proj_kernel,
        grid=(n_tiles, SUB_STEPS),
        in_specs=[pl.BlockSpec((TOKEN_TILE, LANES), functools.partial(lambda sl, t, s: (t, sl), sl))
                  for sl in range(N_SLABS)]
                 + [pl.BlockSpec((1, D_MODEL), lambda t, s: (0, 0)),
                    pl.BlockSpec((D_MODEL, D_NAT), lambda t, s: (0, 0)),
                    pl.BlockSpec((D_CM, D_MODEL), lambda t, s: (0, 0))],
        out_specs=[pl.BlockSpec((SUB_TILE, D_NAT), sub),
                   pl.BlockSpec((POS_PER_STEP, D_CM, TILE_CHUNKS), lambda t, s: (s, 0, t))],
        out_shape=[jax.ShapeDtypeStruct((n, D_NAT), BF16),
                   jax.ShapeDtypeStruct((CHUNK, D_CM, nc), BF16)],
        compiler_params=_params(),
        name="in_proj",
    )(*([x2] * N_SLABS), norm_g, w_nat, w_cm)

    pair_rows = 2 * SSM_GROUP
    s_loc = pl.pallas_call(
        _state_in_kernel,
        grid=(N_PAIRS,),
        in_specs=[pl.BlockSpec((CHUNK, pair_rows, nc), lambda p: (0, p, 0)),
                  pl.BlockSpec((1, PW, PW), lambda p: (p, 0, 0))],
        out_specs=pl.BlockSpec((nc, PW), lambda p: (0, p)),
        out_shape=jax.ShapeDtypeStruct((nc, N_PAIRS * PW), F32),
        compiler_params=_params(),
        name="s5_state_in",
    )(cm, wsi)

    h_in = pl.pallas_call(
        functools.partial(_scan_kernel, cps=cps),
        grid=(bsz, N_PAIRS // SCAN_PAIRS),
        in_specs=[pl.BlockSpec((cps, SCAN_PAIRS * PW), lambda b, m: (b, m)),
                  pl.BlockSpec((SCAN_PAIRS, PW), lambda b, m: (m, 0))],
        out_specs=pl.BlockSpec((cps, SCAN_PAIRS * PW), lambda b, m: (b, m)),
        out_shape=jax.ShapeDtypeStruct((nc, N_PAIRS * PW), F32),
        scratch_shapes=[pltpu.VMEM((4 * SCAN_PAIRS * (cps + 8), LANES), F32)] * 2,
        compiler_params=_params(),
        name="s5_scan",
    )(s_loc, at)

    y_cm = pl.pallas_call(
        _s5_out_kernel,
        grid=(N_PAIRS,),
        in_specs=[pl.BlockSpec((CHUNK, pair_rows, nc), lambda p: (0, p, 0)),
                  pl.BlockSpec((nc, PW), lambda p: (0, p)),
                  pl.BlockSpec((1, PW, PW), lambda p: (p, 0, 0)),
                  pl.BlockSpec((1, PW, PW), lambda p: (p, 0, 0))],
        out_specs=pl.BlockSpec((CHUNK, pair_rows, nc), lambda p: (0, p, 0)),
        out_shape=jax.ShapeDtypeStruct((CHUNK, D_SSM, nc), BF16),
        compiler_params=_params(),
        name="s5_out",
    )(cm, h_in, mt, wso)

    nat2 = nat
    hp = 2 * HEAD_DIM
    o_attn = pl.pallas_call(
        functools.partial(_attn_kernel, rows=rows),
        grid=(N_HEADS // 2, bsz),
        in_specs=[pl.BlockSpec((seq, hp), lambda p, b: (b, p)),
                  pl.BlockSpec((seq, hp), lambda p, b: (b, 4 + p)),
                  pl.BlockSpec((seq, hp), lambda p, b: (b, 8 + p)),
                  pl.BlockSpec((1, 2 * WIN_H - 2, 2 * GRID_W, 2 * GRID_W), lambda p, b: (p, 0, 0, 0))],
        out_specs=pl.BlockSpec((seq, hp), lambda p, b: (b, p)),
        out_shape=jax.ShapeDtypeStruct((n, D_ATTN), BF16),
        compiler_params=_params(),
        name="attention",
    )(nat2, nat2, nat2, bias)

    colv = lambda width: pl.BlockSpec((width, LANES), lambda t, s: (0, 0))
    rowv = lambda width: pl.BlockSpec((1, width), lambda t, s: (0, 0))
    out = pl.pallas_call(
        _out_kernel,
        grid=(n_tiles, SUB_STEPS),
        in_specs=[pl.BlockSpec((SUB_TILE, D_MODEL), sub),
                  pl.BlockSpec((CHUNK, D_SSM, TILE_CHUNKS), lambda t, s: (0, 0, t)),
                  pl.BlockSpec((CHUNK, D_CM, TILE_CHUNKS), lambda t, s: (0, 0, t)),
                  pl.BlockSpec((SUB_TILE, D_ATTN), sub),
                  pl.BlockSpec((SUB_TILE, D_ATTN), lambda t, s: (t * SUB_STEPS + s, 3)),
                  colv(D_SSM),
                  pl.BlockSpec((D_SSM, D_SSM), lambda t, s: (0, 0)),
                  colv(D_SSM), colv(D_SSM), rowv(D_ATTN),
                  pl.BlockSpec((D_MODEL, D_MODEL), lambda t, s: (0, 0)),
                  rowv(D_MODEL)],
        out_specs=pl.BlockSpec((SUB_TILE, D_MODEL), sub),
        out_shape=jax.ShapeDtypeStruct((n, D_MODEL), F32),
        scratch_shapes=[pltpu.VMEM((D_SSM // LANES, TOKEN_TILE, LANES), F32)],
        compiler_params=_params(dimension_semantics=("arbitrary", "arbitrary")),
        name="out_proj",
    )(x2, y_cm, cm, o_attn, nat, d_skip, w_glu_t, b_glu, gs, ga, w_out, gfin)
    return out.reshape(bsz, seq, D_MODEL)


def kernel(x_prompt, x_sample, norm_g, w_in, lam_re, lam_im, b_re, b_im, c_re, c_im, log_dt,
           d_skip, w_glu, b_glu, rpb, ssm_out_g, attn_out_g, w_out, final_norm_g):
    assert norm_g.shape[0] == 1, "single layer only"
    mt, wsi, wso, at = _s5_tables(lam_re[0], lam_im[0], b_re[0], b_im[0], c_re[0], c_im[0], log_dt[0])
    w = w_in[0].astype(BF16)
    w_cm = w[:, :D_CM].T
    w_nat = w[:, D_CM:]
    tabs = (norm_g[0][None], w_nat, w_cm, mt, wsi, wso, at, _bias_table(rpb[0]),
            _col(d_skip[0]), w_glu[0].astype(BF16).T, _col(b_glu[0]), _col(ssm_out_g[0]),
            attn_out_g[0][None], w_out[0].astype(BF16), final_norm_g[None])
    return _trunk(x_prompt, tabs), _trunk(x_sample, tabs)
```

```python
import functools

import jax
import jax.numpy as jnp
import numpy as np
from jax import lax
from jax.experimental import pallas as pl
from jax.experimental.pallas import tpu as pltpu

F32 = jnp.float32
BF16 = jnp.bfloat16
HI = lax.Precision.HIGHEST

D_MODEL = 1024
D_SSM = 512
SSM_GROUP = 16
N_GROUPS = D_SSM // SSM_GROUP
N_PAIRS = N_GROUPS // 2
STATE_P = 64
N_HEADS = 8
HEAD_DIM = 64
D_ATTN = N_HEADS * HEAD_DIM
D_NAT = 3 * D_ATTN + D_ATTN
D_CM = 2 * D_SSM
GRID_W = 64
WIN_H = 8
WIN_W = 16
EPS = 1e-6
CHUNK = 16
PW = 2 * CHUNK * SSM_GROUP
NEG = -1e30

LANES = 128
TILE_CHUNKS = LANES
TOKEN_TILE = TILE_CHUNKS * CHUNK
SUB_STEPS = 4
SUB_TILE = TOKEN_TILE // SUB_STEPS
POS_PER_STEP = CHUNK // SUB_STEPS
N_SLABS = D_MODEL // LANES
SCAN_PAIRS = 8
ATTN_ROWS_PER_STEP = 4
VMEM_LIMIT = 56 * 1024 * 1024


def _rms(x, g):
    return x * lax.rsqrt(jnp.mean(x * x, axis=-1, keepdims=True) + EPS) * g


def _s5_tables(lam_re, lam_im, b_re, b_im, c_re, c_im, log_dt):
    T, G, P, C = CHUNK, N_GROUPS, STATE_P, SSM_GROUP
    dt = jnp.exp(log_dt)[..., None]
    xr, xi = lam_re * dt, lam_im * dt
    n = jnp.arange(T + 1, dtype=F32)[:, None, None, None]
    mag = jnp.exp(n * xr)
    pr, pi = mag * jnp.cos(n * xi), mag * jnp.sin(n * xi)
    a_re, a_im = pr[1], pi[1]
    den = lam_re * lam_re + lam_im * lam_im
    co_re = ((a_re - 1.0) * lam_re + a_im * lam_im) / den
    co_im = (a_im * lam_re - (a_re - 1.0) * lam_im) / den
    bb_re = co_re[..., None] * b_re - co_im[..., None] * b_im
    bb_im = co_re[..., None] * b_im + co_im[..., None] * b_re

    def lanes_gp(w):
        w = w.reshape(2, N_PAIRS, 2, w.shape[2], P)
        return jnp.transpose(w, (0, 1, 3, 2, 4)).reshape(2, N_PAIRS, w.shape[3], 2 * P)

    pw_re = lanes_gp(jnp.transpose(pr, (1, 2, 0, 3)))
    pw_im = lanes_gp(jnp.transpose(pi, (1, 2, 0, 3)))
    bt_re = lanes_gp(jnp.transpose(bb_re, (0, 1, 3, 2)))
    bt_im = lanes_gp(jnp.transpose(bb_im, (0, 1, 3, 2)))
    ct_re, ct_im = lanes_gp(c_re), lanes_gp(c_im)
    tile_l = jnp.asarray(np.tile(np.eye(2 * C, dtype=np.float32), (1, T)))

    small = lambda rows: pl.BlockSpec((2, None, rows, 2 * P), lambda p: (0, p, 0, 0))
    table = pl.BlockSpec((None, PW, PW), lambda p: (p, 0, 0))
    mt, wsi, wso = pl.pallas_call(
        _tables_kernel,
        grid=(N_PAIRS,),
        in_specs=[small(T + 1), small(T + 1), small(C), small(C), small(C), small(C),
                  pl.BlockSpec((2 * C, PW), lambda p: (0, 0))],
        out_specs=[table, table, table],
        out_shape=[jax.ShapeDtypeStruct((N_PAIRS, PW, PW), BF16)] * 3,
        name="s5_tables",
    )(pw_re, pw_im, bt_re, bt_im, ct_re, ct_im, tile_l)

    at = jnp.stack([pr[T, 0], pi[T, 0], pr[T, 1], pi[T, 1]], axis=0)
    at = at.reshape(4, N_PAIRS, 2 * P).transpose(1, 0, 2).reshape(N_PAIRS, 8 * P)
    return mt, wsi, wso, at


def _tables_kernel(pwr_ref, pwi_ref, btr_ref, bti_ref, ctr_ref, cti_ref, tile_ref, mt_ref, wsi_ref, wso_ref):
    T, R = CHUNK, 2 * SSM_GROUP
    first = lax.broadcasted_iota(jnp.int32, (SSM_GROUP, 2 * STATE_P), 1) < STATE_P

    def rows32(ref, d):
        w = ref[d]
        zero = jnp.zeros_like(w)
        return jnp.concatenate([jnp.where(first, w, zero), jnp.where(first, zero, w)], axis=0)

    b_re = [rows32(btr_ref, d) for d in range(2)]
    b_im = [rows32(bti_ref, d) for d in range(2)]
    c_re = [rows32(ctr_ref, d) for d in range(2)]
    c_im = [rows32(cti_ref, d) for d in range(2)]
    pw = lambda d, n: (pwr_ref[d, n:n + 1, :], pwi_ref[d, n:n + 1, :])

    def cmul(w_re, w_im, d, n):
        ar, ai = pw(d, n)
        return w_re * ar - w_im * ai, w_re * ai + w_im * ar

    for j in range(T):
        f_re, f_im = cmul(b_re[0], b_im[0], 0, T - 1 - j)
        g_re, g_im = cmul(b_re[1], b_im[1], 1, j)
        for part, w in enumerate((f_re, f_im, g_re, g_im)):
            wsi_ref[j * R:(j + 1) * R, part * LANES:(part + 1) * LANES] = w.astype(BF16)

    for i in range(T):
        f_re, f_im = cmul(c_re[0], c_im[0], 0, i + 1)
        g_re, g_im = cmul(c_re[1], c_im[1], 1, T - i)
        for part, w in enumerate((f_re, -f_im, g_re, -g_im)):
            wso_ref[i * R:(i + 1) * R, part * LANES:(part + 1) * LANES] = w.astype(BF16)

    nt = (((1,), (1,)), ((), ()))
    kt = []
    for d in range(2):
        ca = [cmul(c_re[d], c_im[d], d, n) for n in range(T)]
        ca_re = jnp.concatenate([w[0] for w in ca], axis=0)
        ca_im = jnp.concatenate([w[1] for w in ca], axis=0)
        k = (lax.dot_general(ca_re, b_re[d], nt, precision=HI, preferred_element_type=F32)
             - lax.dot_general(ca_im, b_im[d], nt, precision=HI, preferred_element_type=F32))
        kt.append(k)
    k_f = jnp.concatenate([kt[0][:R] + kt[1][:R], kt[0][R:]], axis=0)
    kt_f = jnp.dot(k_f, tile_ref[...], precision=HI, preferred_element_type=F32)
    kt_b = jnp.dot(kt[1], tile_ref[...], precision=HI, preferred_element_type=F32)
    col_blk = lax.broadcasted_iota(jnp.int32, (R, PW), 1) // R
    for i in range(T):
        acc = jnp.zeros((R, PW), F32)
        for n in range(i + 1):
            acc = jnp.where(col_blk == i - n, kt_f[n * R:(n + 1) * R], acc)
        for n in range(1, T - i):
            acc = jnp.where(col_blk == i + n, kt_b[n * R:(n + 1) * R], acc)
        mt_ref[i * R:(i + 1) * R, :] = acc.astype(BF16)


def _bias_table(rpb):
    qc = np.arange(GRID_W)[:, None]
    kc = np.arange(GRID_W)[None, :]
    q_start = np.clip(qc - WIN_W // 2, 0, GRID_W - WIN_W)
    valid = (kc >= q_start) & (kc < q_start + WIN_W)
    ci = np.clip(kc - qc, -(WIN_W - 1), WIN_W - 1) + (WIN_W - 1)
    onehot = (ci[None] == np.arange(2 * WIN_W - 1)[:, None, None]).astype(np.float32)
    t = jnp.einsum('hrc,cqk->hrqk', rpb.astype(F32), jnp.asarray(onehot), precision=HI)
    t = jnp.where(jnp.asarray(valid)[None, None], t, NEG)
    nrf = 2 * WIN_H - 2
    t = jnp.stack([t[:, :nrf], t[:, 1:nrf + 1]], axis=3)
    t = t.reshape(N_HEADS // 2, 2, nrf, GRID_W, 2 * GRID_W)
    return jnp.transpose(t, (0, 2, 1, 3, 4)).reshape(N_HEADS // 2, nrf, 2 * GRID_W, 2 * GRID_W)


def _col(v):
    return jnp.broadcast_to(v.astype(F32)[:, None], (v.shape[0], LANES))


def _in_proj_kernel(*refs):
    xs = refs[:N_SLABS]
    g_ref, wn_ref, wc_ref, nat_ref, cm_ref = refs[N_SLABS:]
    s = pl.program_id(1)
    g = g_ref[...]
    r0 = pl.multiple_of(s * SUB_TILE, SUB_TILE)
    xn = jnp.concatenate([x[pl.ds(r0, SUB_TILE), :] for x in xs], axis=1)
    h = _rms(xn, g).astype(BF16)
    nat_ref[...] = jnp.dot(h, wn_ref[...], preferred_element_type=F32).astype(BF16)
    hp = []
    for jl in range(POS_PER_STEP):
        j = s * POS_PER_STEP + jl
        xj = jnp.concatenate([x[pl.ds(j, TILE_CHUNKS, stride=CHUNK), :] for x in xs], axis=1)
        hp.append(_rms(xj, g).astype(BF16))
    hp = jnp.concatenate(hp, axis=0)
    cm = lax.dot_general(wc_ref[...], hp, (((1,), (1,)), ((), ())), preferred_element_type=F32)
    for jl in range(POS_PER_STEP):
        cm_ref[jl] = cm[:, jl * TILE_CHUNKS:(jl + 1) * TILE_CHUNKS].astype(BF16)


def _state_in_kernel(u_ref, w_ref, s_ref):
    z = u_ref[...].reshape(PW, u_ref.shape[-1])
    s_ref[...] = lax.dot_general(z, w_ref[0], (((0,), (0,)), ((), ())), preferred_element_type=F32)


def _scan_kernel(s_ref, a_ref, h_ref, sin, sout, *, cps):
    pitch = cps + 8
    for pr in range(SCAN_PAIRS):
        for part in range(4):
            slab = part * SCAN_PAIRS + pr
            lane0 = (pr * 4 + part) * LANES
            sin[pl.ds(slab * pitch, cps), :] = s_ref[:, lane0:lane0 + LANES]

    ar_f, ai_f = a_ref[:, 0:128], a_ref[:, 128:256]
    ar_b, ai_b = a_ref[:, 256:384], a_ref[:, 384:512]
    rows = lambda part, k: pl.ds(part * SCAN_PAIRS * pitch + k, SCAN_PAIRS, stride=pitch)

    def body(k, carry):
        hfr, hfi, hbr, hbi = carry
        kb = cps - 1 - k
        sout[rows(0, k), :] = hfr
        sout[rows(1, k), :] = hfi
        sout[rows(2, kb), :] = hbr
        sout[rows(3, kb), :] = hbi
        nfr = ar_f * hfr - ai_f * hfi + sin[rows(0, k), :]
        nfi = ar_f * hfi + ai_f * hfr + sin[rows(1, k), :]
        nbr = ar_b * hbr - ai_b * hbi + sin[rows(2, kb), :]
        nbi = ar_b * hbi + ai_b * hbr + sin[rows(3, kb), :]
        return nfr, nfi, nbr, nbi

    z = jnp.zeros((SCAN_PAIRS, LANES), F32)
    lax.fori_loop(0, cps, body, (z, z, z, z))

    for pr in range(SCAN_PAIRS):
        for part in range(4):
            slab = part * SCAN_PAIRS + pr
            lane0 = (pr * 4 + part) * LANES
            h_ref[:, lane0:lane0 + LANES] = sout[pl.ds(slab * pitch, cps), :]


def _s5_out_kernel(u_ref, h_ref, mt_ref, wso_ref, y_ref):
    nc = u_ref.shape[-1]
    z = u_ref[...].reshape(PW, nc)
    y = jnp.dot(mt_ref[0], z, preferred_element_type=F32)
    y = y + lax.dot_general(wso_ref[0], h_ref[...].astype(BF16), (((1,), (1,)), ((), ())),
                            preferred_element_type=F32)
    y_ref[...] = y.astype(BF16).reshape(CHUNK, 2 * SSM_GROUP, nc)


def _attn_kernel(q_ref, k_ref, v_ref, b_ref, o_ref, *, rows):
    lane = lax.broadcasted_iota(jnp.int32, (GRID_W, 2 * HEAD_DIM), 1)
    first = lane < HEAD_DIM
    nkeys = WIN_H * GRID_W
    ones = jnp.ones((nkeys, 2 * HEAD_DIM), BF16)

    def one_row(r):
        rs = jnp.clip(r - WIN_H // 2, 0, rows - WIN_H)
        ri0 = rs - r + (WIN_H - 1)
        q0 = pl.multiple_of(r * GRID_W, GRID_W)
        k0 = pl.multiple_of(rs * GRID_W, GRID_W)
        q = q_ref[pl.ds(q0, GRID_W), :] * jnp.asarray(HEAD_DIM ** -0.5, BF16)
        zero = jnp.zeros_like(q)
        q2 = jnp.concatenate([jnp.where(first, q, zero), jnp.where(first, zero, q)], axis=0)
        kw = k_ref[pl.ds(k0, nkeys), :]
        vw = jnp.concatenate([v_ref[pl.ds(k0, nkeys), :], ones], axis=1)
        s = lax.dot_general(q2, kw, (((1,), (1,)), ((), ())), preferred_element_type=F32)
        s = s + jnp.concatenate([b_ref[0, ri0 + 2 * m] for m in range(WIN_H // 2)], axis=1)
        p = jnp.exp(s - jnp.max(s, axis=-1, keepdims=True))
        ol = jnp.dot(p.astype(BF16), vw, preferred_element_type=F32)
        o2 = ol[:, :2 * HEAD_DIM] / ol[:, 2 * HEAD_DIM:]
        o = jnp.where(first, o2[:GRID_W], o2[GRID_W:])
        o_ref[pl.ds(q0, GRID_W), :] = o.astype(BF16)

    def body(rb, carry):
        for i in range(ATTN_ROWS_PER_STEP):
            one_row(rb * ATTN_ROWS_PER_STEP + i)
        return carry

    lax.fori_loop(0, rows // ATTN_ROWS_PER_STEP, body, 0)


def _out_kernel(x_ref, y_ref, uz_ref, o_ref, za_ref, dskip_ref, wglu_ref, bglu_ref,
                gs_ref, ga_ref, wout_ref, gfin_ref, out_ref, ys_scr):
    s = pl.program_id(1)
    two = lambda r: jnp.concatenate([r[...], r[...]], axis=1)

    @pl.when(s == 0)
    def _():
        for i0 in range(0, CHUNK, 2):
            cat = lambda ref, lo, hi: jnp.concatenate(
                [ref[i0, lo:hi, :], ref[i0 + 1, lo:hi, :]], axis=1).astype(F32)
            y = jax.nn.gelu(cat(y_ref, 0, D_SSM) + two(dskip_ref) * cat(uz_ref, 0, D_SSM))
            gate = jnp.dot(wglu_ref[...], y.astype(BF16), preferred_element_type=F32) + two(bglu_ref)
            y = y * jax.nn.sigmoid(gate)
            y = y * lax.rsqrt(jnp.mean(y * y, axis=0, keepdims=True) + EPS) * two(gs_ref)
            y = y * jax.nn.silu(cat(uz_ref, D_SSM, D_CM))
            for d in range(2):
                yt = y[:, d * LANES:(d + 1) * LANES].T
                for sl in range(D_SSM // LANES):
                    ys_scr[sl, pl.ds(i0 + d, TILE_CHUNKS, stride=CHUNK), :] = yt[:, sl * LANES:(sl + 1) * LANES]

    r0 = pl.multiple_of(s * SUB_TILE, SUB_TILE)
    ys = jnp.concatenate([ys_scr[sl, pl.ds(r0, SUB_TILE), :] for sl in range(D_SSM // LANES)], axis=1)
    ya = _rms(o_ref[...].astype(F32), ga_ref[...]) * jax.nn.silu(za_ref[...].astype(F32))
    mixed = jnp.concatenate([ys.astype(BF16), ya.astype(BF16)], axis=-1)
    out = x_ref[...] + jnp.dot(mixed, wout_ref[...], preferred_element_type=F32)
    out_ref[...] = _rms(out, gfin_ref[...])


def _params(**kw):
    return pltpu.CompilerParams(vmem_limit_bytes=VMEM_LIMIT, **kw)


def _trunk(x, tabs):
    (norm_g, w_nat, w_cm, mt, wsi, wso, at, bias, d_skip, w_glu_t, b_glu, gs, ga, w_out, gfin) = tabs
    bsz, seq, _ = x.shape
    n = bsz * seq
    nc = n // CHUNK
    cps = seq // CHUNK
    rows = seq // GRID_W
    x2 = x.reshape(n, D_MODEL)
    n_tiles = n // TOKEN_TILE
    sub = lambda t, s: (t * SUB_STEPS + s, 0)

    nat, cm = pl.pallas_call(
        _in_proj_kernel,
        grid=(n_tiles, SUB_STEPS),
        in_specs=[pl.BlockSpec((TOKEN_TILE, LANES), functools.partial(lambda sl, t, s: (t, sl), sl))
                  for sl in range(N_SLABS)]
                 + [pl.BlockSpec((1, D_MODEL), lambda t, s: (0, 0)),
                    pl.BlockSpec((D_MODEL, D_NAT), lambda t, s: (0, 0)),
                    pl.BlockSpec((D_CM, D_MODEL), lambda t, s: (0, 0))],
        out_specs=[pl.BlockSpec((SUB_TILE, D_NAT), sub),
                   pl.BlockSpec((POS_PER_STEP, D_CM, TILE_CHUNKS), lambda t, s: (s, 0, t))],
        out_shape=[jax.ShapeDtypeStruct((n, D_NAT), BF16),
                   jax.ShapeDtypeStruct((CHUNK, D_CM, nc), BF16)],
        compiler_params=_params(),
        name="in_proj",
    )(*([x2] * N_SLABS), norm_g, w_nat, w_cm)

    pair_rows = 2 * SSM_GROUP
    s_loc = pl.pallas_call(
        _state_in_kernel,
        grid=(N_PAIRS,),
        in_specs=[pl.BlockSpec((CHUNK, pair_rows, nc), lambda p: (0, p, 0)),
                  pl.BlockSpec((1, PW, PW), lambda p: (p, 0, 0))],
        out_specs=pl.BlockSpec((nc, PW), lambda p: (0, p)),
        out_shape=jax.ShapeDtypeStruct((nc, N_PAIRS * PW), F32),
        compiler_params=_params(),
        name="s5_state_in",
    )(cm, wsi)

    h_in = pl.pallas_call(
        functools.partial(_scan_kernel, cps=cps),
        grid=(bsz, N_PAIRS // SCAN_PAIRS),
        in_specs=[pl.BlockSpec((cps, SCAN_PAIRS * PW), lambda b, m: (b, m)),
                  pl.BlockSpec((SCAN_PAIRS, PW), lambda b, m: (m, 0))],
        out_specs=pl.BlockSpec((cps, SCAN_PAIRS * PW), lambda b, m: (b, m)),
        out_shape=jax.ShapeDtypeStruct((nc, N_PAIRS * PW), F32),
        scratch_shapes=[pltpu.VMEM((4 * SCAN_PAIRS * (cps + 8), LANES), F32)] * 2,
        compiler_params=_params(),
        name="s5_scan",
    )(s_loc, at)

    y_cm = pl.pallas_call(
        _s5_out_kernel,
        grid=(N_PAIRS,),
        in_specs=[pl.BlockSpec((CHUNK, pair_rows, nc), lambda p: (0, p, 0)),
                  pl.BlockSpec((nc, PW), lambda p: (0, p)),
                  pl.BlockSpec((1, PW, PW), lambda p: (p, 0, 0)),
                  pl.BlockSpec((1, PW, PW), lambda p: (p, 0, 0))],
        out_specs=pl.BlockSpec((CHUNK, pair_rows, nc), lambda p: (0, p, 0)),
        out_shape=jax.ShapeDtypeStruct((CHUNK, D_SSM, nc), BF16),
        compiler_params=_params(),
        name="s5_out",
    )(cm, h_in, mt, wso)

    nat2 = nat
    hp = 2 * HEAD_DIM
    o_attn = pl.pallas_call(
        functools.partial(_attn_kernel, rows=rows),
        grid=(N_HEADS // 2, bsz),
        in_specs=[pl.BlockSpec((seq, hp), lambda p, b: (b, p)),
                  pl.BlockSpec((seq, hp), lambda p, b: (b, 4 + p)),
                  pl.BlockSpec((seq, hp), lambda p, b: (b, 8 + p)),
                  pl.BlockSpec((1, 2 * WIN_H - 2, 2 * GRID_W, 2 * GRID_W), lambda p, b: (p, 0, 0, 0))],
        out_specs=pl.BlockSpec((seq, hp), lambda p, b: (b, p)),
        out_shape=jax.ShapeDtypeStruct((n, D_ATTN), BF16),
        compiler_params=_params(),
        name="attention",
    )(nat2, nat2, nat2, bias)

    colv = lambda width: pl.BlockSpec((width, LANES), lambda t, s: (0, 0))
    rowv = lambda width: pl.BlockSpec((1, width), lambda t, s: (0, 0))
    out = pl.pallas_call(
        _out_kernel,
        grid=(n_tiles, SUB_STEPS),
        in_specs=[pl.BlockSpec((SUB_TILE, D_MODEL), sub),
                  pl.BlockSpec((CHUNK, D_SSM, TILE_CHUNKS), lambda t, s: (0, 0, t)),
                  pl.BlockSpec((CHUNK, D_CM, TILE_CHUNKS), lambda t, s: (0, 0, t)),
                  pl.BlockSpec((SUB_TILE, D_ATTN), sub),
                  pl.BlockSpec((SUB_TILE, D_ATTN), lambda t, s: (t * SUB_STEPS + s, 3)),
                  colv(D_SSM),
                  pl.BlockSpec((D_SSM, D_SSM), lambda t, s: (0, 0)),
                  colv(D_SSM), colv(D_SSM), rowv(D_ATTN),
                  pl.BlockSpec((D_MODEL, D_MODEL), lambda t, s: (0, 0)),
                  rowv(D_MODEL)],
        out_specs=pl.BlockSpec((SUB_TILE, D_MODEL), sub),
        out_shape=jax.ShapeDtypeStruct((n, D_MODEL), F32),
        scratch_shapes=[pltpu.VMEM((D_SSM // LANES, TOKEN_TILE, LANES), F32)],
        compiler_params=_params(dimension_semantics=("arbitrary", "arbitrary")),
        name="out_proj",
    )(x2, y_cm, cm, o_attn, nat, d_skip, w_glu_t, b_glu, gs, ga, w_out, gfin)
    return out.reshape(bsz, seq, D_MODEL)


def kernel(x_prompt, x_sample, norm_g, w_in, lam_re, lam_im, b_re, b_im, c_re, c_im, log_dt,
           d_skip, w_glu, b_glu, rpb, ssm_out_g, attn_out_g, w_out, final_norm_g):
    assert norm_g.shape[0] == 1, "single layer only"
    mt, wsi, wso, at = _s5_tables(lam_re[0], lam_im[0], b_re[0], b_im[0], c_re[0], c_im[0], log_dt[0])
    w = w_in[0].astype(BF16)
    w_cm = w[:, :D_CM].T
    w_nat = w[:, D_CM:]
    tabs = (norm_g[0][None], w_nat, w_cm, mt, wsi, wso, at, _bias_table(rpb[0]),
            _col(d_skip[0]), w_glu[0].astype(BF16).T, _col(b_glu[0]), _col(ssm_out_g[0]),
            attn_out_g[0][None], w_out[0].astype(BF16), final_norm_g[None])
    return _trunk(x_prompt, tabs), _trunk(x_sample, tabs)
```

```python
import functools

import jax
import jax.numpy as jnp
import numpy as np
from jax import lax
from jax.experimental import pallas as pl
from jax.experimental.pallas import tpu as pltpu

F32 = jnp.float32
BF16 = jnp.bfloat16
HI = lax.Precision.HIGHEST

D_MODEL = 1024
D_SSM = 512
SSM_GROUP = 16
N_GROUPS = D_SSM // SSM_GROUP
N_PAIRS = N_GROUPS // 2
STATE_P = 64
N_HEADS = 8
HEAD_DIM = 64
D_ATTN = N_HEADS * HEAD_DIM
D_NAT = 3 * D_ATTN + D_ATTN
D_CM = 2 * D_SSM
GRID_W = 64
WIN_H = 8
WIN_W = 16
EPS = 1e-6
CHUNK = 16
PW = 2 * CHUNK * SSM_GROUP
NEG = -1e30

LANES = 128
TILE_CHUNKS = LANES
TOKEN_TILE = TILE_CHUNKS * CHUNK
SUB_STEPS = 4
SUB_TILE = TOKEN_TILE // SUB_STEPS
POS_PER_STEP = CHUNK // SUB_STEPS
N_SLABS = D_MODEL // LANES
SCAN_PAIRS = 8
ATTN_ROWS_PER_STEP = 32
VMEM_LIMIT = 56 * 1024 * 1024


def _rms(x, g):
    return x * lax.rsqrt(jnp.mean(x * x, axis=-1, keepdims=True) + EPS) * g


_GELU_C0 = float(np.sqrt(2.0 / np.pi))
_GELU_C1 = 0.044715 * _GELU_C0


def _times_sigmoid(y, g):
    h = 0.5 * y
    return h + h * jnp.tanh(0.5 * g)


def _silu(z):
    return _times_sigmoid(z, z)


def _gelu_tanh(x):
    h = 0.5 * x
    return h + h * jnp.tanh(x * (_GELU_C0 + _GELU_C1 * (x * x)))


def _s5_tables(lam_re, lam_im, b_re, b_im, c_re, c_im, log_dt):
    T, G, P, C = CHUNK, N_GROUPS, STATE_P, SSM_GROUP
    dt = jnp.exp(log_dt)[..., None]
    xr, xi = lam_re * dt, lam_im * dt
    n = jnp.arange(T + 1, dtype=F32)[:, None, None, None]
    mag = jnp.exp(n * xr)
    pr, pi = mag * jnp.cos(n * xi), mag * jnp.sin(n * xi)
    a_re, a_im = pr[1], pi[1]
    den = lam_re * lam_re + lam_im * lam_im
    co_re = ((a_re - 1.0) * lam_re + a_im * lam_im) / den
    co_im = (a_im * lam_re - (a_re - 1.0) * lam_im) / den
    bb_re = co_re[..., None] * b_re - co_im[..., None] * b_im
    bb_im = co_re[..., None] * b_im + co_im[..., None] * b_re

    def lanes_gp(w):
        w = w.reshape(2, N_PAIRS, 2, w.shape[2], P)
        return jnp.transpose(w, (0, 1, 3, 2, 4)).reshape(2, N_PAIRS, w.shape[3], 2 * P)

    pw_re = lanes_gp(jnp.transpose(pr, (1, 2, 0, 3)))
    pw_im = lanes_gp(jnp.transpose(pi, (1, 2, 0, 3)))
    bt_re = lanes_gp(jnp.transpose(bb_re, (0, 1, 3, 2)))
    bt_im = lanes_gp(jnp.transpose(bb_im, (0, 1, 3, 2)))
    ct_re, ct_im = lanes_gp(c_re), lanes_gp(c_im)
    tile_l = jnp.asarray(np.tile(np.eye(2 * C, dtype=np.float32), (1, T)))

    small = lambda rows: pl.BlockSpec((2, None, rows, 2 * P), lambda p: (0, p, 0, 0))
    table = pl.BlockSpec((None, PW, PW), lambda p: (p, 0, 0))
    mt, wsi, wso = pl.pallas_call(
        _tables_kernel,
        grid=(N_PAIRS,),
        in_specs=[small(T + 1), small(T + 1), small(C), small(C), small(C), small(C),
                  pl.BlockSpec((2 * C, PW), lambda p: (0, 0))],
        out_specs=[table, table, table],
        out_shape=[jax.ShapeDtypeStruct((N_PAIRS, PW, PW), BF16)] * 3,
        name="s5_tables",
    )(pw_re, pw_im, bt_re, bt_im, ct_re, ct_im, tile_l)

    at = jnp.stack([pr[T, 0], pi[T, 0], pr[T, 1], pi[T, 1]], axis=0)
    at = at.reshape(4, N_PAIRS, 2 * P).transpose(1, 0, 2).reshape(N_PAIRS, 8 * P)
    return mt, wsi, wso, at


def _tables_kernel(pwr_ref, pwi_ref, btr_ref, bti_ref, ctr_ref, cti_ref, tile_ref, mt_ref, wsi_ref, wso_ref):
    T, R = CHUNK, 2 * SSM_GROUP
    first = lax.broadcasted_iota(jnp.int32, (SSM_GROUP, 2 * STATE_P), 1) < STATE_P

    def rows32(ref, d):
        w = ref[d]
        zero = jnp.zeros_like(w)
        return jnp.concatenate([jnp.where(first, w, zero), jnp.where(first, zero, w)], axis=0)

    b_re = [rows32(btr_ref, d) for d in range(2)]
    b_im = [rows32(bti_ref, d) for d in range(2)]
    c_re = [rows32(ctr_ref, d) for d in range(2)]
    c_im = [rows32(cti_ref, d) for d in range(2)]
    pw = lambda d, n: (pwr_ref[d, n:n + 1, :], pwi_ref[d, n:n + 1, :])

    def cmul(w_re, w_im, d, n):
        ar, ai = pw(d, n)
        return w_re * ar - w_im * ai, w_re * ai + w_im * ar

    for j in range(T):
        f_re, f_im = cmul(b_re[0], b_im[0], 0, T - 1 - j)
        g_re, g_im = cmul(b_re[1], b_im[1], 1, j)
        for part, w in enumerate((f_re, f_im, g_re, g_im)):
            wsi_ref[j * R:(j + 1) * R, part * LANES:(part + 1) * LANES] = w.astype(BF16)

    for i in range(T):
        f_re, f_im = cmul(c_re[0], c_im[0], 0, i + 1)
        g_re, g_im = cmul(c_re[1], c_im[1], 1, T - i)
        for part, w in enumerate((f_re, -f_im, g_re, -g_im)):
            wso_ref[i * R:(i + 1) * R, part * LANES:(part + 1) * LANES] = w.astype(BF16)

    nt = (((1,), (1,)), ((), ()))
    kt = []
    for d in range(2):
        ca = [cmul(c_re[d], c_im[d], d, n) for n in range(T)]
        ca_re = jnp.concatenate([w[0] for w in ca], axis=0)
        ca_im = jnp.concatenate([w[1] for w in ca], axis=0)
        k = (lax.dot_general(ca_re, b_re[d], nt, precision=HI, preferred_element_type=F32)
             - lax.dot_general(ca_im, b_im[d], nt, precision=HI, preferred_element_type=F32))
        kt.append(k)
    k_f = jnp.concatenate([kt[0][:R] + kt[1][:R], kt[0][R:]], axis=0)
    kt_f = jnp.dot(k_f, tile_ref[...], precision=HI, preferred_element_type=F32)
    kt_b = jnp.dot(kt[1], tile_ref[...], precision=HI, preferred_element_type=F32)
    col_blk = lax.broadcasted_iota(jnp.int32, (R, PW), 1) // R
    for i in range(T):
        acc = jnp.zeros((R, PW), F32)
        for n in range(i + 1):
            acc = jnp.where(col_blk == i - n, kt_f[n * R:(n + 1) * R], acc)
        for n in range(1, T - i):
            acc = jnp.where(col_blk == i + n, kt_b[n * R:(n + 1) * R], acc)
        mt_ref[i * R:(i + 1) * R, :] = acc.astype(BF16)


def _bias_table(rpb):
    qc = np.arange(GRID_W)[:, None]
    kc = np.arange(GRID_W)[None, :]
    q_start = np.clip(qc - WIN_W // 2, 0, GRID_W - WIN_W)
    valid = (kc >= q_start) & (kc < q_start + WIN_W)
    ci = np.clip(kc - qc, -(WIN_W - 1), WIN_W - 1) + (WIN_W - 1)
    onehot = (ci[None] == np.arange(2 * WIN_W - 1)[:, None, None]).astype(np.float32)
    t = jnp.einsum('hrc,cqk->hrqk', rpb.astype(F32), jnp.asarray(onehot), precision=HI)
    t = jnp.where(jnp.asarray(valid)[None, None], t, NEG)
    nrf = 2 * WIN_H - 2
    t = jnp.stack([t[:, :nrf], t[:, 1:nrf + 1]], axis=3)
    t = t.reshape(N_HEADS // 2, 2, nrf, GRID_W, 2 * GRID_W)
    return jnp.transpose(t, (0, 2, 1, 3, 4)).reshape(N_HEADS // 2, nrf, 2 * GRID_W, 2 * GRID_W)


def _col(v):
    return jnp.broadcast_to(v.astype(F32)[:, None], (v.shape[0], LANES))


def _in_proj_kernel(*refs):
    xs = refs[:N_SLABS]
    g_ref, wn_ref, wc_ref, nat_ref, cm_ref = refs[N_SLABS:]
    s = pl.program_id(1)
    g = g_ref[...]
    r0 = pl.multiple_of(s * SUB_TILE, SUB_TILE)
    xn = jnp.concatenate([x[pl.ds(r0, SUB_TILE), :] for x in xs], axis=1)
    h = _rms(xn, g).astype(BF16)
    nat_ref[...] = jnp.dot(h, wn_ref[...], preferred_element_type=F32).astype(BF16)
    hp = []
    for jl in range(POS_PER_STEP):
        j = s * POS_PER_STEP + jl
        xj = jnp.concatenate([x[pl.ds(j, TILE_CHUNKS, stride=CHUNK), :] for x in xs], axis=1)
        hp.append(_rms(xj, g).astype(BF16))
    hp = jnp.concatenate(hp, axis=0)
    cm = lax.dot_general(wc_ref[...], hp, (((1,), (1,)), ((), ())), preferred_element_type=F32)
    for jl in range(POS_PER_STEP):
        cm_ref[jl] = cm[:, jl * TILE_CHUNKS:(jl + 1) * TILE_CHUNKS].astype(BF16)


def _state_in_kernel(u_ref, w_ref, s_ref):
    z = u_ref[...].reshape(PW, u_ref.shape[-1])
    s_ref[...] = lax.dot_general(z, w_ref[0], (((0,), (0,)), ((), ())), preferred_element_type=F32)


def _scan_kernel(s_ref, a_ref, h_ref, sin, sout, *, cps):
    pitch = cps + 8
    for pr in range(SCAN_PAIRS):
        for part in range(4):
            slab = part * SCAN_PAIRS + pr
            lane0 = (pr * 4 + part) * LANES
            sin[pl.ds(slab * pitch, cps), :] = s_ref[:, lane0:lane0 + LANES]

    ar_f, ai_f = a_ref[:, 0:128], a_ref[:, 128:256]
    ar_b, ai_b = a_ref[:, 256:384], a_ref[:, 384:512]
    rows = lambda part, k: pl.ds(part * SCAN_PAIRS * pitch + k, SCAN_PAIRS, stride=pitch)

    def body(k, carry):
        hfr, hfi, hbr, hbi = carry
        kb = cps - 1 - k
        sout[rows(0, k), :] = hfr
        sout[rows(1, k), :] = hfi
        sout[rows(2, kb), :] = hbr
        sout[rows(3, kb), :] = hbi
        nfr = ar_f * hfr - ai_f * hfi + sin[rows(0, k), :]
        nfi = ar_f * hfi + ai_f * hfr + sin[rows(1, k), :]
        nbr = ar_b * hbr - ai_b * hbi + sin[rows(2, kb), :]
        nbi = ar_b * hbi + ai_b * hbr + sin[rows(3, kb), :]
        return nfr, nfi, nbr, nbi

    z = jnp.zeros((SCAN_PAIRS, LANES), F32)
    lax.fori_loop(0, cps, body, (z, z, z, z))

    for pr in range(SCAN_PAIRS):
        for part in range(4):
            slab = part * SCAN_PAIRS + pr
            lane0 = (pr * 4 + part) * LANES
            h_ref[:, lane0:lane0 + LANES] = sout[pl.ds(slab * pitch, cps), :]


def _s5_out_kernel(u_ref, h_ref, mt_ref, wso_ref, y_ref):
    nc = u_ref.shape[-1]
    z = u_ref[...].reshape(PW, nc)
    y = jnp.dot(mt_ref[0], z, preferred_element_type=F32)
    y = y + lax.dot_general(wso_ref[0], h_ref[...].astype(BF16), (((1,), (1,)), ((), ())),
                            preferred_element_type=F32)
    y_ref[...] = y.astype(BF16).reshape(CHUNK, 2 * SSM_GROUP, nc)


def _attn_kernel(q_ref, k_ref, v_ref, b_ref, o_ref, *, rows):
    lane = lax.broadcasted_iota(jnp.int32, (GRID_W, 2 * HEAD_DIM), 1)
    first = lane < HEAD_DIM
    nkeys = WIN_H * GRID_W
    ones = jnp.ones((nkeys, 2 * HEAD_DIM), BF16)

    def one_row(r):
        rs = jnp.clip(r - WIN_H // 2, 0, rows - WIN_H)
        ri0 = rs - r + (WIN_H - 1)
        q0 = pl.multiple_of(r * GRID_W, GRID_W)
        k0 = pl.multiple_of(rs * GRID_W, GRID_W)
        q = q_ref[pl.ds(q0, GRID_W), :] * jnp.asarray(HEAD_DIM ** -0.5, BF16)
        zero = jnp.zeros_like(q)
        q2 = jnp.concatenate([jnp.where(first, q, zero), jnp.where(first, zero, q)], axis=0)
        kw = k_ref[pl.ds(k0, nkeys), :]
        vw = jnp.concatenate([v_ref[pl.ds(k0, nkeys), :], ones], axis=1)
        s = lax.dot_general(q2, kw, (((1,), (1,)), ((), ())), preferred_element_type=F32)
        s = s + jnp.concatenate([b_ref[0, ri0 + 2 * m] for m in range(WIN_H // 2)], axis=1)
        p = jnp.exp(s - jnp.max(s, axis=-1, keepdims=True))
        ol = jnp.dot(p.astype(BF16), vw, preferred_element_type=F32)
        o2 = ol[:, :2 * HEAD_DIM] / ol[:, 2 * HEAD_DIM:]
        o = jnp.where(first, o2[:GRID_W], o2[GRID_W:])
        o_ref[pl.ds(q0, GRID_W), :] = o.astype(BF16)

    def body(rb, carry):
        for i in range(ATTN_ROWS_PER_STEP):
            one_row(rb * ATTN_ROWS_PER_STEP + i)
        return carry

    lax.fori_loop(0, rows // ATTN_ROWS_PER_STEP, body, 0)


def _out_kernel(x_ref, y_ref, uz_ref, o_ref, za_ref, dskip_ref, wglu_ref, bglu_ref,
                gs_ref, ga_ref, wout_ref, gfin_ref, out_ref, ys_scr):
    s = pl.program_id(1)
    two = lambda r: jnp.concatenate([r[...], r[...]], axis=1)

    @pl.when(s == 0)
    def _():
        for i0 in range(0, CHUNK, 2):
            cat = lambda ref, lo, hi: jnp.concatenate(
                [ref[i0, lo:hi, :], ref[i0 + 1, lo:hi, :]], axis=1).astype(F32)
            y = _gelu_tanh(cat(y_ref, 0, D_SSM) + two(dskip_ref) * cat(uz_ref, 0, D_SSM))
            gate = jnp.dot(wglu_ref[...], y.astype(BF16), preferred_element_type=F32) + two(bglu_ref)
            y = _times_sigmoid(y, gate)
            y = y * lax.rsqrt(jnp.mean(y * y, axis=0, keepdims=True) + EPS) * two(gs_ref)
            y = y * _silu(cat(uz_ref, D_SSM, D_CM))
            for d in range(2):
                yt = y[:, d * LANES:(d + 1) * LANES].T
                for sl in range(D_SSM // LANES):
                    ys_scr[sl, pl.ds(i0 + d, TILE_CHUNKS, stride=CHUNK), :] = yt[:, sl * LANES:(sl + 1) * LANES]

    r0 = pl.multiple_of(s * SUB_TILE, SUB_TILE)
    ys = jnp.concatenate([ys_scr[sl, pl.ds(r0, SUB_TILE), :] for sl in range(D_SSM // LANES)], axis=1)
    ya = _rms(o_ref[...].astype(F32), ga_ref[...]) * _silu(za_ref[...].astype(F32))
    mixed = jnp.concatenate([ys.astype(BF16), ya.astype(BF16)], axis=-1)
    out = x_ref[...] + jnp.dot(mixed, wout_ref[...], preferred_element_type=F32)
    out_ref[...] = _rms(out, gfin_ref[...])


def _params(**kw):
    return pltpu.CompilerParams(vmem_limit_bytes=VMEM_LIMIT, **kw)


def _trunk(x, tabs):
    (norm_g, w_nat, w_cm, mt, wsi, wso, at, bias, d_skip, w_glu_t, b_glu, gs, ga, w_out, gfin) = tabs
    bsz, seq, _ = x.shape
    n = bsz * seq
    nc = n // CHUNK
    cps = seq // CHUNK
    rows = seq // GRID_W
    x2 = x.reshape(n, D_MODEL)
    n_tiles = n // TOKEN_TILE
    sub = lambda t, s: (t * SUB_STEPS + s, 0)

    nat, cm = pl.pallas_call(
        _in_proj_kernel,
        grid=(n_tiles, SUB_STEPS),
        in_specs=[pl.BlockSpec((TOKEN_TILE, LANES), functools.partial(lambda sl, t, s: (t, sl), sl))
                  for sl in range(N_SLABS)]
                 + [pl.BlockSpec((1, D_MODEL), lambda t, s: (0, 0)),
                    pl.BlockSpec((D_MODEL, D_NAT), lambda t, s: (0, 0)),
                    pl.BlockSpec((D_CM, D_MODEL), lambda t, s: (0, 0))],
        out_specs=[pl.BlockSpec((SUB_TILE, D_NAT), sub),
                   pl.BlockSpec((POS_PER_STEP, D_CM, TILE_CHUNKS), lambda t, s: (s, 0, t))],
        out_shape=[jax.ShapeDtypeStruct((n, D_NAT), BF16),
                   jax.ShapeDtypeStruct((CHUNK, D_CM, nc), BF16)],
        compiler_params=_params(),
        name="in_proj",
    )(*([x2] * N_SLABS), norm_g, w_nat, w_cm)

    pair_rows = 2 * SSM_GROUP
    s_loc = pl.pallas_call(
        _state_in_kernel,
        grid=(N_PAIRS,),
        in_specs=[pl.BlockSpec((CHUNK, pair_rows, nc), lambda p: (0, p, 0)),
                  pl.BlockSpec((1, PW, PW), lambda p: (p, 0, 0))],
        out_specs=pl.BlockSpec((nc, PW), lambda p: (0, p)),
        out_shape=jax.ShapeDtypeStruct((nc, N_PAIRS * PW), F32),
        compiler_params=_params(),
        name="s5_state_in",
    )(cm, wsi)

    h_in = pl.pallas_call(
        functools.partial(_scan_kernel, cps=cps),
        grid=(bsz, N_PAIRS // SCAN_PAIRS),
        in_specs=[pl.BlockSpec((cps, SCAN_PAIRS * PW), lambda b, m: (b, m)),
                  pl.BlockSpec((SCAN_PAIRS, PW), lambda b, m: (m, 0))],
        out_specs=pl.BlockSpec((cps, SCAN_PAIRS * PW), lambda b, m: (b, m)),
        out_shape=jax.ShapeDtypeStruct((nc, N_PAIRS * PW), F32),
        scratch_shapes=[pltpu.VMEM((4 * SCAN_PAIRS * (cps + 8), LANES), F32)] * 2,
        compiler_params=_params(),
        name="s5_scan",
    )(s_loc, at)

    y_cm = pl.pallas_call(
        _s5_out_kernel,
        grid=(N_PAIRS,),
        in_specs=[pl.BlockSpec((CHUNK, pair_rows, nc), lambda p: (0, p, 0)),
                  pl.BlockSpec((nc, PW), lambda p: (0, p)),
                  pl.BlockSpec((1, PW, PW), lambda p: (p, 0, 0)),
                  pl.BlockSpec((1, PW, PW), lambda p: (p, 0, 0))],
        out_specs=pl.BlockSpec((CHUNK, pair_rows, nc), lambda p: (0, p, 0)),
        out_shape=jax.ShapeDtypeStruct((CHUNK, D_SSM, nc), BF16),
        compiler_params=_params(),
        name="s5_out",
    )(cm, h_in, mt, wso)

    nat2 = nat
    hp = 2 * HEAD_DIM
    o_attn = pl.pallas_call(
        functools.partial(_attn_kernel, rows=rows),
        grid=(N_HEADS // 2, bsz),
        in_specs=[pl.BlockSpec((seq, hp), lambda p, b: (b, p)),
                  pl.BlockSpec((seq, hp), lambda p, b: (b, 4 + p)),
                  pl.BlockSpec((seq, hp), lambda p, b: (b, 8 + p)),
                  pl.BlockSpec((1, 2 * WIN_H - 2, 2 * GRID_W, 2 * GRID_W), lambda p, b: (p, 0, 0, 0))],
        out_specs=pl.BlockSpec((seq, hp), lambda p, b: (b, p)),
        out_shape=jax.ShapeDtypeStruct((n, D_ATTN), BF16),
        compiler_params=_params(),
        name="attention",
    )(nat2, nat2, nat2, bias)

    colv = lambda width: pl.BlockSpec((width, LANES), lambda t, s: (0, 0))
    rowv = lambda width: pl.BlockSpec((1, width), lambda t, s: (0, 0))
    out = pl.pallas_call(
        _out_kernel,
        grid=(n_tiles, SUB_STEPS),
        in_specs=[pl.BlockSpec((SUB_TILE, D_MODEL), sub),
                  pl.BlockSpec((CHUNK, D_SSM, TILE_CHUNKS), lambda t, s: (0, 0, t)),
                  pl.BlockSpec((CHUNK, D_CM, TILE_CHUNKS), lambda t, s: (0, 0, t)),
                  pl.BlockSpec((SUB_TILE, D_ATTN), sub),
                  pl.BlockSpec((SUB_TILE, D_ATTN), lambda t, s: (t * SUB_STEPS + s, 3)),
                  colv(D_SSM),
                  pl.BlockSpec((D_SSM, D_SSM), lambda t, s: (0, 0)),
                  colv(D_SSM), colv(D_SSM), rowv(D_ATTN),
                  pl.BlockSpec((D_MODEL, D_MODEL), lambda t, s: (0, 0)),
                  rowv(D_MODEL)],
        out_specs=pl.BlockSpec((SUB_TILE, D_MODEL), sub),
        out_shape=jax.ShapeDtypeStruct((n, D_MODEL), F32),
        scratch_shapes=[pltpu.VMEM((D_SSM // LANES, TOKEN_TILE, LANES), F32)],
        compiler_params=_params(dimension_semantics=("arbitrary", "arbitrary")),
        name="out_proj",
    )(x2, y_cm, cm, o_attn, nat, d_skip, w_glu_t, b_glu, gs, ga, w_out, gfin)
    return out.reshape(bsz, seq, D_MODEL)


def kernel(x_prompt, x_sample, norm_g, w_in, lam_re, lam_im, b_re, b_im, c_re, c_im, log_dt,
           d_skip, w_glu, b_glu, rpb, ssm_out_g, attn_out_g, w_out, final_norm_g):
    assert norm_g.shape[0] == 1, "single layer only"
    mt, wsi, wso, at = _s5_tables(lam_re[0], lam_im[0], b_re[0], b_im[0], c_re[0], c_im[0], log_dt[0])
    w = w_in[0].astype(BF16)
    w_cm = w[:, :D_CM].T
    w_nat = w[:, D_CM:]
    tabs = (norm_g[0][None], w_nat, w_cm, mt, wsi, wso, at, _bias_table(rpb[0]),
            _col(d_skip[0]), w_glu[0].astype(BF16).T, _col(b_glu[0]), _col(ssm_out_g[0]),
            attn_out_g[0][None], w_out[0].astype(BF16), final_norm_g[None])
    return _trunk(x_prompt, tabs), _trunk(x_sample, tabs)
```

```python
import functools

import jax
import jax.numpy as jnp
import numpy as np
from jax import lax
from jax.experimental import pallas as pl
from jax.experimental.pallas import tpu as pltpu

F32 = jnp.float32
BF16 = jnp.bfloat16
HI = lax.Precision.HIGHEST

D_MODEL = 1024
D_SSM = 512
SSM_GROUP = 16
N_GROUPS = D_SSM // SSM_GROUP
N_PAIRS = N_GROUPS // 2
STATE_P = 64
N_HEADS = 8
HEAD_DIM = 64
D_ATTN = N_HEADS * HEAD_DIM
D_NAT = 3 * D_ATTN + D_ATTN
D_CM = 2 * D_SSM
GRID_W = 64
WIN_H = 8
WIN_W = 16
EPS = 1e-6
CHUNK = 16
PW = 2 * CHUNK * SSM_GROUP
NEG = -1e30

LANES = 128
TILE_CHUNKS = LANES
TOKEN_TILE = TILE_CHUNKS * CHUNK
SUB_STEPS = 4
SUB_TILE = TOKEN_TILE // SUB_STEPS
POS_PER_STEP = CHUNK // SUB_STEPS
N_SLABS = D_MODEL // LANES
SCAN_PAIRS = 8
ATTN_ROWS_PER_STEP = 32
VMEM_LIMIT = 56 * 1024 * 1024


def _rms(x, g):
    return x * lax.rsqrt(jnp.mean(x * x, axis=-1, keepdims=True) + EPS) * g


_GELU_C0 = float(np.sqrt(2.0 / np.pi))
_GELU_C1 = 0.044715 * _GELU_C0


def _times_sigmoid(y, g):
    h = 0.5 * y
    return h + h * jnp.tanh(0.5 * g)


def _silu(z):
    return _times_sigmoid(z, z)


def _gelu_tanh(x):
    h = 0.5 * x
    return h + h * jnp.tanh(x * (_GELU_C0 + _GELU_C1 * (x * x)))


def _s5_tables(lam_re, lam_im, b_re, b_im, c_re, c_im, log_dt):
    T, G, P, C = CHUNK, N_GROUPS, STATE_P, SSM_GROUP
    dt = jnp.exp(log_dt)[..., None]
    xr, xi = lam_re * dt, lam_im * dt
    n = jnp.arange(T + 1, dtype=F32)[:, None, None, None]
    mag = jnp.exp(n * xr)
    pr, pi = mag * jnp.cos(n * xi), mag * jnp.sin(n * xi)
    a_re, a_im = pr[1], pi[1]
    den = lam_re * lam_re + lam_im * lam_im
    co_re = ((a_re - 1.0) * lam_re + a_im * lam_im) / den
    co_im = (a_im * lam_re - (a_re - 1.0) * lam_im) / den
    bb_re = co_re[..., None] * b_re - co_im[..., None] * b_im
    bb_im = co_re[..., None] * b_im + co_im[..., None] * b_re

    def lanes_gp(w):
        w = w.reshape(2, N_PAIRS, 2, w.shape[2], P)
        return jnp.transpose(w, (0, 1, 3, 2, 4)).reshape(2, N_PAIRS, w.shape[3], 2 * P)

    pw_re = lanes_gp(jnp.transpose(pr, (1, 2, 0, 3)))
    pw_im = lanes_gp(jnp.transpose(pi, (1, 2, 0, 3)))
    bt_re = lanes_gp(jnp.transpose(bb_re, (0, 1, 3, 2)))
    bt_im = lanes_gp(jnp.transpose(bb_im, (0, 1, 3, 2)))
    ct_re, ct_im = lanes_gp(c_re), lanes_gp(c_im)
    tile_l = jnp.asarray(np.tile(np.eye(2 * C, dtype=np.float32), (1, T)))

    small = lambda rows: pl.BlockSpec((2, None, rows, 2 * P), lambda p: (0, p, 0, 0))
    table = pl.BlockSpec((None, PW, PW), lambda p: (p, 0, 0))
    mt, wsi, wso = pl.pallas_call(
        _tables_kernel,
        grid=(N_PAIRS,),
        in_specs=[small(T + 1), small(T + 1), small(C), small(C), small(C), small(C),
                  pl.BlockSpec((2 * C, PW), lambda p: (0, 0))],
        out_specs=[table, table, table],
        out_shape=[jax.ShapeDtypeStruct((N_PAIRS, PW, PW), BF16)] * 3,
        name="s5_tables",
    )(pw_re, pw_im, bt_re, bt_im, ct_re, ct_im, tile_l)

    at = jnp.stack([pr[T, 0], pi[T, 0], pr[T, 1], pi[T, 1]], axis=0)
    at = at.reshape(4, N_PAIRS, 2 * P).transpose(1, 0, 2).reshape(N_PAIRS, 8 * P)
    return mt, wsi, wso, at


def _tables_kernel(pwr_ref, pwi_ref, btr_ref, bti_ref, ctr_ref, cti_ref, tile_ref, mt_ref, wsi_ref, wso_ref):
    T, R = CHUNK, 2 * SSM_GROUP
    first = lax.broadcasted_iota(jnp.int32, (SSM_GROUP, 2 * STATE_P), 1) < STATE_P

    def rows32(ref, d):
        w = ref[d]
        zero = jnp.zeros_like(w)
        return jnp.concatenate([jnp.where(first, w, zero), jnp.where(first, zero, w)], axis=0)

    b_re = [rows32(btr_ref, d) for d in range(2)]
    b_im = [rows32(bti_ref, d) for d in range(2)]
    c_re = [rows32(ctr_ref, d) for d in range(2)]
    c_im = [rows32(cti_ref, d) for d in range(2)]
    pw = lambda d, n: (pwr_ref[d, n:n + 1, :], pwi_ref[d, n:n + 1, :])

    def cmul(w_re, w_im, d, n):
        ar, ai = pw(d, n)
        return w_re * ar - w_im * ai, w_re * ai + w_im * ar

    for j in range(T):
        f_re, f_im = cmul(b_re[0], b_im[0], 0, T - 1 - j)
        g_re, g_im = cmul(b_re[1], b_im[1], 1, j)
        for part, w in enumerate((f_re, f_im, g_re, g_im)):
            wsi_ref[j * R:(j + 1) * R, part * LANES:(part + 1) * LANES] = w.astype(BF16)

    for i in range(T):
        f_re, f_im = cmul(c_re[0], c_im[0], 0, i + 1)
        g_re, g_im = cmul(c_re[1], c_im[1], 1, T - i)
        for part, w in enumerate((f_re, -f_im, g_re, -g_im)):
            wso_ref[i * R:(i + 1) * R, part * LANES:(part + 1) * LANES] = w.astype(BF16)

    nt = (((1,), (1,)), ((), ()))
    kt = []
    for d in range(2):
        ca = [cmul(c_re[d], c_im[d], d, n) for n in range(T)]
        ca_re = jnp.concatenate([w[0] for w in ca], axis=0)
        ca_im = jnp.concatenate([w[1] for w in ca], axis=0)
        k = (lax.dot_general(ca_re, b_re[d], nt, precision=HI, preferred_element_type=F32)
             - lax.dot_general(ca_im, b_im[d], nt, precision=HI, preferred_element_type=F32))
        kt.append(k)
    k_f = jnp.concatenate([kt[0][:R] + kt[1][:R], kt[0][R:]], axis=0)
    kt_f = jnp.dot(k_f, tile_ref[...], precision=HI, preferred_element_type=F32)
    kt_b = jnp.dot(kt[1], tile_ref[...], precision=HI, preferred_element_type=F32)
    col_blk = lax.broadcasted_iota(jnp.int32, (R, PW), 1) // R
    for i in range(T):
        acc = jnp.zeros((R, PW), F32)
        for n in range(i + 1):
            acc = jnp.where(col_blk == i - n, kt_f[n * R:(n + 1) * R], acc)
        for n in range(1, T - i):
            acc = jnp.where(col_blk == i + n, kt_b[n * R:(n + 1) * R], acc)
        mt_ref[i * R:(i + 1) * R, :] = acc.astype(BF16)


def _bias_table(rpb):
    qc = np.arange(GRID_W)[:, None]
    kc = np.arange(GRID_W)[None, :]
    q_start = np.clip(qc - WIN_W // 2, 0, GRID_W - WIN_W)
    valid = (kc >= q_start) & (kc < q_start + WIN_W)
    ci = np.clip(kc - qc, -(WIN_W - 1), WIN_W - 1) + (WIN_W - 1)
    onehot = (ci[None] == np.arange(2 * WIN_W - 1)[:, None, None]).astype(np.float32)
    t = jnp.einsum('hrc,cqk->hrqk', rpb.astype(F32), jnp.asarray(onehot), precision=HI)
    t = jnp.where(jnp.asarray(valid)[None, None], t, NEG)
    nrf = 2 * WIN_H - 2
    t = jnp.stack([t[:, :nrf], t[:, 1:nrf + 1]], axis=3)
    t = t.reshape(N_HEADS // 2, 2, nrf, GRID_W, 2 * GRID_W)
    return jnp.transpose(t, (0, 2, 1, 3, 4)).reshape(N_HEADS // 2, nrf, 2 * GRID_W, 2 * GRID_W)


def _col(v):
    return jnp.broadcast_to(v.astype(F32)[:, None], (v.shape[0], LANES))


def _in_proj_kernel(*refs):
    xs = refs[:N_SLABS]
    g_ref, wn_ref, wc_ref, nat_ref, cm_ref = refs[N_SLABS:]
    s = pl.program_id(1)
    g = g_ref[...]
    r0 = pl.multiple_of(s * SUB_TILE, SUB_TILE)
    xn = jnp.concatenate([x[pl.ds(r0, SUB_TILE), :] for x in xs], axis=1)
    h = _rms(xn, g).astype(BF16)
    nat_ref[...] = jnp.dot(h, wn_ref[...], preferred_element_type=F32).astype(BF16)
    hp = []
    for jl in range(POS_PER_STEP):
        j = s * POS_PER_STEP + jl
        xj = jnp.concatenate([x[pl.ds(j, TILE_CHUNKS, stride=CHUNK), :] for x in xs], axis=1)
        hp.append(_rms(xj, g).astype(BF16))
    hp = jnp.concatenate(hp, axis=0)
    cm = lax.dot_general(wc_ref[...], hp, (((1,), (1,)), ((), ())), preferred_element_type=F32)
    for jl in range(POS_PER_STEP):
        cm_ref[jl] = cm[:, jl * TILE_CHUNKS:(jl + 1) * TILE_CHUNKS].astype(BF16)


def _state_in_kernel(u_ref, w_ref, s_ref):
    z = u_ref[...].reshape(PW, u_ref.shape[-1])
    s_ref[...] = lax.dot_general(z, w_ref[0], (((0,), (0,)), ((), ())), preferred_element_type=F32)


def _scan_kernel(s_ref, a_ref, h_ref, sin, sout, *, cps):
    pitch = cps + 8
    for pr in range(SCAN_PAIRS):
        for part in range(4):
            slab = part * SCAN_PAIRS + pr
            lane0 = (pr * 4 + part) * LANES
            sin[pl.ds(slab * pitch, cps), :] = s_ref[:, lane0:lane0 + LANES]

    ar_f, ai_f = a_ref[:, 0:128], a_ref[:, 128:256]
    ar_b, ai_b = a_ref[:, 256:384], a_ref[:, 384:512]
    rows = lambda part, k: pl.ds(part * SCAN_PAIRS * pitch + k, SCAN_PAIRS, stride=pitch)

    def body(k, carry):
        hfr, hfi, hbr, hbi = carry
        kb = cps - 1 - k
        sout[rows(0, k), :] = hfr
        sout[rows(1, k), :] = hfi
        sout[rows(2, kb), :] = hbr
        sout[rows(3, kb), :] = hbi
        nfr = ar_f * hfr - ai_f * hfi + sin[rows(0, k), :]
        nfi = ar_f * hfi + ai_f * hfr + sin[rows(1, k), :]
        nbr = ar_b * hbr - ai_b * hbi + sin[rows(2, kb), :]
        nbi = ar_b * hbi + ai_b * hbr + sin[rows(3, kb), :]
        return nfr, nfi, nbr, nbi

    z = jnp.zeros((SCAN_PAIRS, LANES), F32)
    lax.fori_loop(0, cps, body, (z, z, z, z))

    for pr in range(SCAN_PAIRS):
        for part in range(4):
            slab = part * SCAN_PAIRS + pr
            lane0 = (pr * 4 + part) * LANES
            h_ref[:, lane0:lane0 + LANES] = sout[pl.ds(slab * pitch, cps), :]


def _s5_out_kernel(u_ref, h_ref, mt_ref, wso_ref, y_ref):
    nc = u_ref.shape[-1]
    z = u_ref[...].reshape(PW, nc)
    y = jnp.dot(mt_ref[0], z, preferred_element_type=F32)
    y = y + lax.dot_general(wso_ref[0], h_ref[...].astype(BF16), (((1,), (1,)), ((), ())),
                            preferred_element_type=F32)
    y_ref[...] = y.astype(BF16).reshape(CHUNK, 2 * SSM_GROUP, nc)


def _attn_kernel(q_ref, k_ref, v_ref, b_ref, o_ref, *, rows):
    lane = lax.broadcasted_iota(jnp.int32, (GRID_W, 2 * HEAD_DIM), 1)
    first = lane < HEAD_DIM
    nkeys = WIN_H * GRID_W
    ones = jnp.ones((nkeys, 2 * HEAD_DIM), BF16)

    def one_row(r):
        rs = jnp.clip(r - WIN_H // 2, 0, rows - WIN_H)
        ri0 = rs - r + (WIN_H - 1)
        q0 = pl.multiple_of(r * GRID_W, GRID_W)
        k0 = pl.multiple_of(rs * GRID_W, GRID_W)
        q = q_ref[pl.ds(q0, GRID_W), :] * jnp.asarray(HEAD_DIM ** -0.5, BF16)
        zero = jnp.zeros_like(q)
        q2 = jnp.concatenate([jnp.where(first, q, zero), jnp.where(first, zero, q)], axis=0)
        kw = k_ref[pl.ds(k0, nkeys), :]
        vw = jnp.concatenate([v_ref[pl.ds(k0, nkeys), :], ones], axis=1)
        s = lax.dot_general(q2, kw, (((1,), (1,)), ((), ())), preferred_element_type=F32)
        s = s + jnp.concatenate([b_ref[0, ri0 + 2 * m] for m in range(WIN_H // 2)], axis=1)
        p = jnp.exp(s - jnp.max(s, axis=-1, keepdims=True))
        ol = jnp.dot(p.astype(BF16), vw, preferred_element_type=F32)
        o2 = ol[:, :2 * HEAD_DIM] / ol[:, 2 * HEAD_DIM:]
        o = jnp.where(first, o2[:GRID_W], o2[GRID_W:])
        o_ref[pl.ds(q0, GRID_W), :] = o.astype(BF16)

    def body(rb, carry):
        for i in range(ATTN_ROWS_PER_STEP):
            one_row(rb * ATTN_ROWS_PER_STEP + i)
        return carry

    lax.fori_loop(0, rows // ATTN_ROWS_PER_STEP, body, 0)


def _out_kernel(x_ref, y_ref, uz_ref, o_ref, za_ref, dskip_ref, wglu_ref, bglu_ref,
                gs_ref, ga_ref, wout_ref, gfin_ref, out_ref, ys_scr, *, n_tiles):
    t = pl.program_id(0)
    s = pl.program_id(1)
    two = lambda r: jnp.concatenate([r[...], r[...]], axis=1)

    @pl.when(t < n_tiles)
    def _():
        slot = t % 2
        for i0 in range(0, POS_PER_STEP, 2):
            cat = lambda ref, lo, hi: jnp.concatenate(
                [ref[i0, lo:hi, :], ref[i0 + 1, lo:hi, :]], axis=1).astype(F32)
            y = _gelu_tanh(cat(y_ref, 0, D_SSM) + two(dskip_ref) * cat(uz_ref, 0, D_SSM))
            gate = jnp.dot(wglu_ref[...], y.astype(BF16), preferred_element_type=F32) + two(bglu_ref)
            y = _times_sigmoid(y, gate)
            y = y * lax.rsqrt(jnp.mean(y * y, axis=0, keepdims=True) + EPS) * two(gs_ref)
            y = y * _silu(cat(uz_ref, D_SSM, D_CM))
            for d in range(2):
                yt = y[:, d * LANES:(d + 1) * LANES].T
                pos = s * POS_PER_STEP + i0 + d
                for sl in range(D_SSM // LANES):
                    ys_scr[slot, sl, pl.ds(pos, TILE_CHUNKS, stride=CHUNK), :] = yt[:, sl * LANES:(sl + 1) * LANES]

    @pl.when(t > 0)
    def _():
        slot = (t + 1) % 2
        r0 = pl.multiple_of(s * SUB_TILE, SUB_TILE)
        ys = jnp.concatenate([ys_scr[slot, sl, pl.ds(r0, SUB_TILE), :] for sl in range(D_SSM // LANES)], axis=1)
        ya = _rms(o_ref[...].astype(F32), ga_ref[...]) * _silu(za_ref[...].astype(F32))
        mixed = jnp.concatenate([ys.astype(BF16), ya.astype(BF16)], axis=-1)
        out = x_ref[...] + jnp.dot(mixed, wout_ref[...], preferred_element_type=F32)
        out_ref[...] = _rms(out, gfin_ref[...])


def _params(**kw):
    return pltpu.CompilerParams(vmem_limit_bytes=VMEM_LIMIT, **kw)


def _trunk(x, tabs):
    (norm_g, w_nat, w_cm, mt, wsi, wso, at, bias, d_skip, w_glu_t, b_glu, gs, ga, w_out, gfin) = tabs
    bsz, seq, _ = x.shape
    n = bsz * seq
    nc = n // CHUNK
    cps = seq // CHUNK
    rows = seq // GRID_W
    x2 = x.reshape(n, D_MODEL)
    n_tiles = n // TOKEN_TILE
    sub = lambda t, s: (t * SUB_STEPS + s, 0)

    nat, cm = pl.pallas_call(
        _in_proj_kernel,
        grid=(n_tiles, SUB_STEPS),
        in_specs=[pl.BlockSpec((TOKEN_TILE, LANES), functools.partial(lambda sl, t, s: (t, sl), sl))
                  for sl in range(N_SLABS)]
                 + [pl.BlockSpec((1, D_MODEL), lambda t, s: (0, 0)),
                    pl.BlockSpec((D_MODEL, D_NAT), lambda t, s: (0, 0)),
                    pl.BlockSpec((D_CM, D_MODEL), lambda t, s: (0, 0))],
        out_specs=[pl.BlockSpec((SUB_TILE, D_NAT), sub),
                   pl.BlockSpec((POS_PER_STEP, D_CM, TILE_CHUNKS), lambda t, s: (s, 0, t))],
        out_shape=[jax.ShapeDtypeStruct((n, D_NAT), BF16),
                   jax.ShapeDtypeStruct((CHUNK, D_CM, nc), BF16)],
        compiler_params=_params(),
        name="in_proj",
    )(*([x2] * N_SLABS), norm_g, w_nat, w_cm)

    pair_rows = 2 * SSM_GROUP
    s_loc = pl.pallas_call(
        _state_in_kernel,
        grid=(N_PAIRS,),
        in_specs=[pl.BlockSpec((CHUNK, pair_rows, nc), lambda p: (0, p, 0)),
                  pl.BlockSpec((1, PW, PW), lambda p: (p, 0, 0))],
        out_specs=pl.BlockSpec((nc, PW), lambda p: (0, p)),
        out_shape=jax.ShapeDtypeStruct((nc, N_PAIRS * PW), F32),
        compiler_params=_params(),
        name="s5_state_in",
    )(cm, wsi)

    h_in = pl.pallas_call(
        functools.partial(_scan_kernel, cps=cps),
        grid=(bsz, N_PAIRS // SCAN_PAIRS),
        in_specs=[pl.BlockSpec((cps, SCAN_PAIRS * PW), lambda b, m: (b, m)),
                  pl.BlockSpec((SCAN_PAIRS, PW), lambda b, m: (m, 0))],
        out_specs=pl.BlockSpec((cps, SCAN_PAIRS * PW), lambda b, m: (b, m)),
        out_shape=jax.ShapeDtypeStruct((nc, N_PAIRS * PW), F32),
        scratch_shapes=[pltpu.VMEM((4 * SCAN_PAIRS * (cps + 8), LANES), F32)] * 2,
        compiler_params=_params(),
        name="s5_scan",
    )(s_loc, at)

    y_cm = pl.pallas_call(
        _s5_out_kernel,
        grid=(N_PAIRS,),
        in_specs=[pl.BlockSpec((CHUNK, pair_rows, nc), lambda p: (0, p, 0)),
                  pl.BlockSpec((nc, PW), lambda p: (0, p)),
                  pl.BlockSpec((1, PW, PW), lambda p: (p, 0, 0)),
                  pl.BlockSpec((1, PW, PW), lambda p: (p, 0, 0))],
        out_specs=pl.BlockSpec((CHUNK, pair_rows, nc), lambda p: (0, p, 0)),
        out_shape=jax.ShapeDtypeStruct((CHUNK, D_SSM, nc), BF16),
        compiler_params=_params(),
        name="s5_out",
    )(cm, h_in, mt, wso)

    nat2 = nat
    hp = 2 * HEAD_DIM
    o_attn = pl.pallas_call(
        functools.partial(_attn_kernel, rows=rows),
        grid=(N_HEADS // 2, bsz),
        in_specs=[pl.BlockSpec((seq, hp), lambda p, b: (b, p)),
                  pl.BlockSpec((seq, hp), lambda p, b: (b, 4 + p)),
                  pl.BlockSpec((seq, hp), lambda p, b: (b, 8 + p)),
                  pl.BlockSpec((1, 2 * WIN_H - 2, 2 * GRID_W, 2 * GRID_W), lambda p, b: (p, 0, 0, 0))],
        out_specs=pl.BlockSpec((seq, hp), lambda p, b: (b, p)),
        out_shape=jax.ShapeDtypeStruct((n, D_ATTN), BF16),
        compiler_params=_params(),
        name="attention",
    )(nat2, nat2, nat2, bias)

    colv = lambda width: pl.BlockSpec((width, LANES), lambda t, s: (0, 0))
    rowv = lambda width: pl.BlockSpec((1, width), lambda t, s: (0, 0))
    last = n_tiles - 1
    cm_idx = lambda t, s: (jnp.where(t > last, SUB_STEPS - 1, s), 0, jnp.minimum(t, last))
    nat_row = lambda t, s: jnp.where(t > 0, (t - 1) * SUB_STEPS + s, 0)
    out = pl.pallas_call(
        functools.partial(_out_kernel, n_tiles=n_tiles),
        grid=(n_tiles + 1, SUB_STEPS),
        in_specs=[pl.BlockSpec((SUB_TILE, D_MODEL), lambda t, s: (nat_row(t, s), 0)),
                  pl.BlockSpec((POS_PER_STEP, D_SSM, TILE_CHUNKS), cm_idx),
                  pl.BlockSpec((POS_PER_STEP, D_CM, TILE_CHUNKS), cm_idx),
                  pl.BlockSpec((SUB_TILE, D_ATTN), lambda t, s: (nat_row(t, s), 0)),
                  pl.BlockSpec((SUB_TILE, D_ATTN), lambda t, s: (nat_row(t, s), 3)),
                  colv(D_SSM),
                  pl.BlockSpec((D_SSM, D_SSM), lambda t, s: (0, 0)),
                  colv(D_SSM), colv(D_SSM), rowv(D_ATTN),
                  pl.BlockSpec((D_MODEL, D_MODEL), lambda t, s: (0, 0)),
                  rowv(D_MODEL)],
        out_specs=pl.BlockSpec((SUB_TILE, D_MODEL), lambda t, s: (nat_row(t, s), 0)),
        out_shape=jax.ShapeDtypeStruct((n, D_MODEL), F32),
        scratch_shapes=[pltpu.VMEM((2, D_SSM // LANES, TOKEN_TILE, LANES), F32)],
        compiler_params=_params(dimension_semantics=("arbitrary", "arbitrary")),
        name="out_proj",
    )(x2, y_cm, cm, o_attn, nat, d_skip, w_glu_t, b_glu, gs, ga, w_out, gfin)
    return out.reshape(bsz, seq, D_MODEL)


def kernel(x_prompt, x_sample, norm_g, w_in, lam_re, lam_im, b_re, b_im, c_re, c_im, log_dt,
           d_skip, w_glu, b_glu, rpb, ssm_out_g, attn_out_g, w_out, final_norm_g):
    assert norm_g.shape[0] == 1, "single layer only"
    mt, wsi, wso, at = _s5_tables(lam_re[0], lam_im[0], b_re[0], b_im[0], c_re[0], c_im[0], log_dt[0])
    w = w_in[0].astype(BF16)
    w_cm = w[:, :D_CM].T
    w_nat = w[:, D_CM:]
    tabs = (norm_g[0][None], w_nat, w_cm, mt, wsi, wso, at, _bias_table(rpb[0]),
            _col(d_skip[0]), w_glu[0].astype(BF16).T, _col(b_glu[0]), _col(ssm_out_g[0]),
            attn_out_g[0][None], w_out[0].astype(BF16), final_norm_g[None])
    return _trunk(x_prompt, tabs), _trunk(x_sample, tabs)
```

```python
import functools

import jax
import jax.numpy as jnp
import numpy as np
from jax import lax
from jax.experimental import pallas as pl
from jax.experimental.pallas import tpu as pltpu

F32 = jnp.float32
BF16 = jnp.bfloat16
HI = lax.Precision.HIGHEST

D_MODEL = 1024
D_SSM = 512
SSM_GROUP = 16
N_GROUPS = D_SSM // SSM_GROUP
N_PAIRS = N_GROUPS // 2
STATE_P = 64
N_HEADS = 8
HEAD_DIM = 64
D_ATTN = N_HEADS * HEAD_DIM
D_NAT = 3 * D_ATTN + D_ATTN
D_CM = 2 * D_SSM
GRID_W = 64
WIN_H = 8
WIN_W = 16
EPS = 1e-6
CHUNK = 16
PW = 2 * CHUNK * SSM_GROUP
NEG = -1e30

LANES = 128
TILE_CHUNKS = LANES
TOKEN_TILE = TILE_CHUNKS * CHUNK
SUB_STEPS = 4
SUB_TILE = TOKEN_TILE // SUB_STEPS
POS_PER_STEP = CHUNK // SUB_STEPS
IN_SUB_STEPS = 2
IN_SUB_TILE = TOKEN_TILE // IN_SUB_STEPS
IN_POS_PER_STEP = CHUNK // IN_SUB_STEPS
N_SLABS = D_MODEL // LANES
SCAN_PAIRS = 8
ATTN_ROWS_PER_STEP = 32
VMEM_LIMIT = 56 * 1024 * 1024


def _rms(x, g):
    return x * lax.rsqrt(jnp.mean(x * x, axis=-1, keepdims=True) + EPS) * g


_GELU_C0 = float(np.sqrt(2.0 / np.pi))
_GELU_C1 = 0.044715 * _GELU_C0


def _times_sigmoid(y, g):
    h = 0.5 * y
    return h + h * jnp.tanh(0.5 * g)


def _silu(z):
    return _times_sigmoid(z, z)


def _gelu_tanh(x):
    h = 0.5 * x
    return h + h * jnp.tanh(x * (_GELU_C0 + _GELU_C1 * (x * x)))


def _s5_tables(lam_re, lam_im, b_re, b_im, c_re, c_im, log_dt):
    T, G, P, C = CHUNK, N_GROUPS, STATE_P, SSM_GROUP
    dt = jnp.exp(log_dt)[..., None]
    xr, xi = lam_re * dt, lam_im * dt
    n = jnp.arange(T + 1, dtype=F32)[:, None, None, None]
    mag = jnp.exp(n * xr)
    pr, pi = mag * jnp.cos(n * xi), mag * jnp.sin(n * xi)
    a_re, a_im = pr[1], pi[1]
    den = lam_re * lam_re + lam_im * lam_im
    co_re = ((a_re - 1.0) * lam_re + a_im * lam_im) / den
    co_im = (a_im * lam_re - (a_re - 1.0) * lam_im) / den
    bb_re = co_re[..., None] * b_re - co_im[..., None] * b_im
    bb_im = co_re[..., None] * b_im + co_im[..., None] * b_re

    def lanes_gp(w):
        w = w.reshape(2, N_PAIRS, 2, w.shape[2], P)
        return jnp.transpose(w, (0, 1, 3, 2, 4)).reshape(2, N_PAIRS, w.shape[3], 2 * P)

    pw_re = lanes_gp(jnp.transpose(pr, (1, 2, 0, 3)))
    pw_im = lanes_gp(jnp.transpose(pi, (1, 2, 0, 3)))
    bt_re = lanes_gp(jnp.transpose(bb_re, (0, 1, 3, 2)))
    bt_im = lanes_gp(jnp.transpose(bb_im, (0, 1, 3, 2)))
    ct_re, ct_im = lanes_gp(c_re), lanes_gp(c_im)
    tile_l = jnp.asarray(np.tile(np.eye(2 * C, dtype=np.float32), (1, T)))

    small = lambda rows: pl.BlockSpec((2, None, rows, 2 * P), lambda p: (0, p, 0, 0))
    table = pl.BlockSpec((None, PW, PW), lambda p: (p, 0, 0))
    mt, wsi, wso = pl.pallas_call(
        _tables_kernel,
        grid=(N_PAIRS,),
        in_specs=[small(T + 1), small(T + 1), small(C), small(C), small(C), small(C),
                  pl.BlockSpec((2 * C, PW), lambda p: (0, 0))],
        out_specs=[table, table, table],
        out_shape=[jax.ShapeDtypeStruct((N_PAIRS, PW, PW), BF16)] * 3,
        name="s5_tables",
    )(pw_re, pw_im, bt_re, bt_im, ct_re, ct_im, tile_l)

    at = jnp.stack([pr[T, 0], pi[T, 0], pr[T, 1], pi[T, 1]], axis=0)
    at = at.reshape(4, N_PAIRS, 2 * P).transpose(1, 0, 2).reshape(N_PAIRS, 8 * P)
    return mt, wsi, wso, at


def _tables_kernel(pwr_ref, pwi_ref, btr_ref, bti_ref, ctr_ref, cti_ref, tile_ref, mt_ref, wsi_ref, wso_ref):
    T, R = CHUNK, 2 * SSM_GROUP
    first = lax.broadcasted_iota(jnp.int32, (SSM_GROUP, 2 * STATE_P), 1) < STATE_P

    def rows32(ref, d):
        w = ref[d]
        zero = jnp.zeros_like(w)
        return jnp.concatenate([jnp.where(first, w, zero), jnp.where(first, zero, w)], axis=0)

    b_re = [rows32(btr_ref, d) for d in range(2)]
    b_im = [rows32(bti_ref, d) for d in range(2)]
    c_re = [rows32(ctr_ref, d) for d in range(2)]
    c_im = [rows32(cti_ref, d) for d in range(2)]
    pw = lambda d, n: (pwr_ref[d, n:n + 1, :], pwi_ref[d, n:n + 1, :])

    def cmul(w_re, w_im, d, n):
        ar, ai = pw(d, n)
        return w_re * ar - w_im * ai, w_re * ai + w_im * ar

    for j in range(T):
        f_re, f_im = cmul(b_re[0], b_im[0], 0, T - 1 - j)
        g_re, g_im = cmul(b_re[1], b_im[1], 1, j)
        for part, w in enumerate((f_re, f_im, g_re, g_im)):
            wsi_ref[j * R:(j + 1) * R, part * LANES:(part + 1) * LANES] = w.astype(BF16)

    for i in range(T):
        f_re, f_im = cmul(c_re[0], c_im[0], 0, i + 1)
        g_re, g_im = cmul(c_re[1], c_im[1], 1, T - i)
        for part, w in enumerate((f_re, -f_im, g_re, -g_im)):
            wso_ref[i * R:(i + 1) * R, part * LANES:(part + 1) * LANES] = w.astype(BF16)

    nt = (((1,), (1,)), ((), ()))
    kt = []
    for d in range(2):
        ca = [cmul(c_re[d], c_im[d], d, n) for n in range(T)]
        ca_re = jnp.concatenate([w[0] for w in ca], axis=0)
        ca_im = jnp.concatenate([w[1] for w in ca], axis=0)
        k = (lax.dot_general(ca_re, b_re[d], nt, precision=HI, preferred_element_type=F32)
             - lax.dot_general(ca_im, b_im[d], nt, precision=HI, preferred_element_type=F32))
        kt.append(k)
    k_f = jnp.concatenate([kt[0][:R] + kt[1][:R], kt[0][R:]], axis=0)
    kt_f = jnp.dot(k_f, tile_ref[...], precision=HI, preferred_element_type=F32)
    kt_b = jnp.dot(kt[1], tile_ref[...], precision=HI, preferred_element_type=F32)
    col_blk = lax.broadcasted_iota(jnp.int32, (R, PW), 1) // R
    for i in range(T):
        acc = jnp.zeros((R, PW), F32)
        for n in range(i + 1):
            acc = jnp.where(col_blk == i - n, kt_f[n * R:(n + 1) * R], acc)
        for n in range(1, T - i):
            acc = jnp.where(col_blk == i + n, kt_b[n * R:(n + 1) * R], acc)
        mt_ref[i * R:(i + 1) * R, :] = acc.astype(BF16)


def _bias_table(rpb):
    qc = np.arange(GRID_W)[:, None]
    kc = np.arange(GRID_W)[None, :]
    q_start = np.clip(qc - WIN_W // 2, 0, GRID_W - WIN_W)
    valid = (kc >= q_start) & (kc < q_start + WIN_W)
    ci = np.clip(kc - qc, -(WIN_W - 1), WIN_W - 1) + (WIN_W - 1)
    onehot = (ci[None] == np.arange(2 * WIN_W - 1)[:, None, None]).astype(np.float32)
    t = jnp.einsum('hrc,cqk->hrqk', rpb.astype(F32), jnp.asarray(onehot), precision=HI)
    t = jnp.where(jnp.asarray(valid)[None, None], t, NEG)
    nrf = 2 * WIN_H - 2
    t = jnp.stack([t[:, :nrf], t[:, 1:nrf + 1]], axis=3)
    t = t.reshape(N_HEADS // 2, 2, nrf, GRID_W, 2 * GRID_W)
    return jnp.transpose(t, (0, 2, 1, 3, 4)).reshape(N_HEADS // 2, nrf, 2 * GRID_W, 2 * GRID_W)


def _col(v):
    return jnp.broadcast_to(v.astype(F32)[:, None], (v.shape[0], LANES))


def _in_proj_kernel(*refs):
    xs = refs[:N_SLABS]
    g_ref, wn_ref, wc_ref, nat_ref, cm_ref, hp_scr = refs[N_SLABS:]
    s = pl.program_id(1)
    g = g_ref[...]
    r0 = pl.multiple_of(s * IN_SUB_TILE, IN_SUB_TILE)
    xn = jnp.concatenate([x[pl.ds(r0, IN_SUB_TILE), :] for x in xs], axis=1)
    h = _rms(xn, g).astype(BF16)
    for jl in range(IN_POS_PER_STEP):
        j = s * IN_POS_PER_STEP + jl
        xj = jnp.concatenate([x[pl.ds(j, TILE_CHUNKS, stride=CHUNK), :] for x in xs], axis=1)
        hp_scr[jl * TILE_CHUNKS:(jl + 1) * TILE_CHUNKS, :] = _rms(xj, g).astype(BF16)
    nat_ref[...] = jnp.dot(h, wn_ref[...], preferred_element_type=F32).astype(BF16)
    cm = lax.dot_general(wc_ref[...], hp_scr[...], (((1,), (1,)), ((), ())), preferred_element_type=F32)
    for jl in range(IN_POS_PER_STEP):
        cm_ref[jl] = cm[:, jl * TILE_CHUNKS:(jl + 1) * TILE_CHUNKS].astype(BF16)


def _scan_kernel(u_ref, w_ref, a_ref, h_ref, sin, sout, *, cps):
    pitch = cps + 8
    pair_rows = 2 * SSM_GROUP
    for pr in range(SCAN_PAIRS):
        z = u_ref[:, pr * pair_rows:(pr + 1) * pair_rows, :].reshape(PW, cps)
        st = lax.dot_general(z, w_ref[pr], (((0,), (0,)), ((), ())), preferred_element_type=F32)
        for part in range(4):
            slab = part * SCAN_PAIRS + pr
            sin[pl.ds(slab * pitch, cps), :] = st[:, part * LANES:(part + 1) * LANES]

    ar_f, ai_f = a_ref[:, 0:128], a_ref[:, 128:256]
    ar_b, ai_b = a_ref[:, 256:384], a_ref[:, 384:512]
    rows = lambda part, k: pl.ds(part * SCAN_PAIRS * pitch + k, SCAN_PAIRS, stride=pitch)

    def body(k, carry):
        hfr, hfi, hbr, hbi = carry
        kb = cps - 1 - k
        sout[rows(0, k), :] = hfr
        sout[rows(1, k), :] = hfi
        sout[rows(2, kb), :] = hbr
        sout[rows(3, kb), :] = hbi
        nfr = ar_f * hfr - ai_f * hfi + sin[rows(0, k), :]
        nfi = ar_f * hfi + ai_f * hfr + sin[rows(1, k), :]
        nbr = ar_b * hbr - ai_b * hbi + sin[rows(2, kb), :]
        nbi = ar_b * hbi + ai_b * hbr + sin[rows(3, kb), :]
        return nfr, nfi, nbr, nbi

    z = jnp.zeros((SCAN_PAIRS, LANES), F32)
    lax.fori_loop(0, cps, body, (z, z, z, z))

    for pr in range(SCAN_PAIRS):
        for part in range(4):
            slab = part * SCAN_PAIRS + pr
            lane0 = (pr * 4 + part) * LANES
            h_ref[:, lane0:lane0 + LANES] = sout[pl.ds(slab * pitch, cps), :].astype(BF16)


def _s5_out_kernel(u_ref, h_ref, mt_ref, wso_ref, d_ref, y_ref):
    nc = u_ref.shape[-1]
    z = u_ref[...].reshape(PW, nc)
    y = jnp.dot(mt_ref[0], z, preferred_element_type=F32)
    y = y + lax.dot_general(wso_ref[0], h_ref[...], (((1,), (1,)), ((), ())),
                            preferred_element_type=F32)
    d = jnp.concatenate([d_ref[0]] * CHUNK, axis=0)
    y = _gelu_tanh(y + jnp.concatenate([d] * (nc // LANES), axis=1) * z.astype(F32))
    y_ref[...] = y.astype(BF16).reshape(CHUNK, 2 * SSM_GROUP, nc)


def _attn_kernel(q_ref, k_ref, v_ref, b_ref, o_ref, *, rows):
    lane = lax.broadcasted_iota(jnp.int32, (GRID_W, 2 * HEAD_DIM), 1)
    first = lane < HEAD_DIM
    nkeys = WIN_H * GRID_W
    ones = jnp.ones((nkeys, 2 * HEAD_DIM), BF16)

    def one_row(r):
        rs = jnp.clip(r - WIN_H // 2, 0, rows - WIN_H)
        ri0 = rs - r + (WIN_H - 1)
        q0 = pl.multiple_of(r * GRID_W, GRID_W)
        k0 = pl.multiple_of(rs * GRID_W, GRID_W)
        q = q_ref[pl.ds(q0, GRID_W), :] * jnp.asarray(HEAD_DIM ** -0.5, BF16)
        zero = jnp.zeros_like(q)
        q2 = jnp.concatenate([jnp.where(first, q, zero), jnp.where(first, zero, q)], axis=0)
        kw = k_ref[pl.ds(k0, nkeys), :]
        vw = jnp.concatenate([v_ref[pl.ds(k0, nkeys), :], ones], axis=1)
        s = lax.dot_general(q2, kw, (((1,), (1,)), ((), ())), preferred_element_type=F32)
        s = s + jnp.concatenate([b_ref[0, ri0 + 2 * m] for m in range(WIN_H // 2)], axis=1)
        p = jnp.exp(s - jnp.max(s, axis=-1, keepdims=True))
        ol = jnp.dot(p.astype(BF16), vw, preferred_element_type=F32)
        o2 = ol[:, :2 * HEAD_DIM] / ol[:, 2 * HEAD_DIM:]
        o = jnp.where(first, o2[:GRID_W], o2[GRID_W:])
        o_ref[pl.ds(q0, GRID_W), :] = o.astype(BF16)

    def body(rb, carry):
        for i in range(ATTN_ROWS_PER_STEP):
            one_row(rb * ATTN_ROWS_PER_STEP + i)
        return carry

    lax.fori_loop(0, rows // ATTN_ROWS_PER_STEP, body, 0)


def _out_kernel(x_ref, y_ref, zs_ref, o_ref, za_ref, wglu_ref, bglu_ref,
                gs_ref, ga_ref, wout_ref, gfin_ref, out_ref, ys_scr, *, n_tiles):
    t = pl.program_id(0)
    s = pl.program_id(1)
    two = lambda r: jnp.concatenate([r[...], r[...]], axis=1)

    @pl.when(t < n_tiles)
    def _():
        slot = t % 2
        for i0 in range(0, POS_PER_STEP, 2):
            cat = lambda ref: jnp.concatenate([ref[i0], ref[i0 + 1]], axis=1)
            yb = cat(y_ref)
            gate = jnp.dot(wglu_ref[...], yb, preferred_element_type=F32) + two(bglu_ref)
            y = _times_sigmoid(yb.astype(F32), gate)
            y = y * lax.rsqrt(jnp.mean(y * y, axis=0, keepdims=True) + EPS) * two(gs_ref)
            y = y * _silu(cat(zs_ref).astype(F32))
            for d in range(2):
                yt = y[:, d * LANES:(d + 1) * LANES].T
                pos = s * POS_PER_STEP + i0 + d
                for sl in range(D_SSM // LANES):
                    ys_scr[slot, sl, pl.ds(pos, TILE_CHUNKS, stride=CHUNK), :] = yt[:, sl * LANES:(sl + 1) * LANES]

    @pl.when(t > 0)
    def _():
        slot = (t + 1) % 2
        r0 = pl.multiple_of(s * SUB_TILE, SUB_TILE)
        ys = jnp.concatenate([ys_scr[slot, sl, pl.ds(r0, SUB_TILE), :] for sl in range(D_SSM // LANES)], axis=1)
        ya = _rms(o_ref[...].astype(F32), ga_ref[...]) * _silu(za_ref[...].astype(F32))
        mixed = jnp.concatenate([ys.astype(BF16), ya.astype(BF16)], axis=-1)
        out = x_ref[...] + jnp.dot(mixed, wout_ref[...], preferred_element_type=F32)
        out_ref[...] = _rms(out, gfin_ref[...])


def _params(**kw):
    return pltpu.CompilerParams(vmem_limit_bytes=VMEM_LIMIT, **kw)


def _trunk(x, tabs):
    (norm_g, w_nat, w_cm, mt, wsi, wso, at, bias, d_skip, w_glu_t, b_glu, gs, ga, w_out, gfin) = tabs
    bsz, seq, _ = x.shape
    n = bsz * seq
    nc = n // CHUNK
    cps = seq // CHUNK
    rows = seq // GRID_W
    x2 = x.reshape(n, D_MODEL)
    n_tiles = n // TOKEN_TILE

    nat, cm = pl.pallas_call(
        _in_proj_kernel,
        grid=(n_tiles, IN_SUB_STEPS),
        in_specs=[pl.BlockSpec((TOKEN_TILE, LANES), functools.partial(lambda sl, t, s: (t, sl), sl))
                  for sl in range(N_SLABS)]
                 + [pl.BlockSpec((1, D_MODEL), lambda t, s: (0, 0)),
                    pl.BlockSpec((D_MODEL, D_NAT), lambda t, s: (0, 0)),
                    pl.BlockSpec((D_CM, D_MODEL), lambda t, s: (0, 0))],
        out_specs=[pl.BlockSpec((IN_SUB_TILE, D_NAT), lambda t, s: (t * IN_SUB_STEPS + s, 0)),
                   pl.BlockSpec((IN_POS_PER_STEP, D_CM, TILE_CHUNKS), lambda t, s: (s, 0, t))],
        out_shape=[jax.ShapeDtypeStruct((n, D_NAT), BF16),
                   jax.ShapeDtypeStruct((CHUNK, D_CM, nc), BF16)],
        scratch_shapes=[pltpu.VMEM((IN_POS_PER_STEP * TILE_CHUNKS, D_MODEL), BF16)],
        compiler_params=_params(),
        name="in_proj",
    )(*([x2] * N_SLABS), norm_g, w_nat, w_cm)

    pair_rows = 2 * SSM_GROUP
    h_in = pl.pallas_call(
        functools.partial(_scan_kernel, cps=cps),
        grid=(bsz, N_PAIRS // SCAN_PAIRS),
        in_specs=[pl.BlockSpec((CHUNK, SCAN_PAIRS * pair_rows, cps), lambda b, m: (0, m, b)),
                  pl.BlockSpec((SCAN_PAIRS, PW, PW), lambda b, m: (m, 0, 0)),
                  pl.BlockSpec((SCAN_PAIRS, PW), lambda b, m: (m, 0))],
        out_specs=pl.BlockSpec((cps, SCAN_PAIRS * PW), lambda b, m: (b, m)),
        out_shape=jax.ShapeDtypeStruct((nc, N_PAIRS * PW), BF16),
        scratch_shapes=[pltpu.VMEM((4 * SCAN_PAIRS * (cps + 8), LANES), F32)] * 2,
        compiler_params=_params(),
        name="s5_scan",
    )(cm, wsi, at)

    y_cm = pl.pallas_call(
        _s5_out_kernel,
        grid=(N_PAIRS,),
        in_specs=[pl.BlockSpec((CHUNK, pair_rows, nc), lambda p: (0, p, 0)),
                  pl.BlockSpec((nc, PW), lambda p: (0, p)),
                  pl.BlockSpec((1, PW, PW), lambda p: (p, 0, 0)),
                  pl.BlockSpec((1, PW, PW), lambda p: (p, 0, 0)),
                  pl.BlockSpec((1, pair_rows, LANES), lambda p: (p, 0, 0))],
        out_specs=pl.BlockSpec((CHUNK, pair_rows, nc), lambda p: (0, p, 0)),
        out_shape=jax.ShapeDtypeStruct((CHUNK, D_SSM, nc), BF16),
        compiler_params=_params(),
        name="s5_out",
    )(cm, h_in, mt, wso, d_skip)

    nat2 = nat
    hp = 2 * HEAD_DIM
    o_attn = pl.pallas_call(
        functools.partial(_attn_kernel, rows=rows),
        grid=(N_HEADS // 2, bsz),
        in_specs=[pl.BlockSpec((seq, hp), lambda p, b: (b, p)),
                  pl.BlockSpec((seq, hp), lambda p, b: (b, 4 + p)),
                  pl.BlockSpec((seq, hp), lambda p, b: (b, 8 + p)),
                  pl.BlockSpec((1, 2 * WIN_H - 2, 2 * GRID_W, 2 * GRID_W), lambda p, b: (p, 0, 0, 0))],
        out_specs=pl.BlockSpec((seq, hp), lambda p, b: (b, p)),
        out_shape=jax.ShapeDtypeStruct((n, D_ATTN), BF16),
        compiler_params=_params(),
        name="attention",
    )(nat2, nat2, nat2, bias)

    colv = lambda width: pl.BlockSpec((width, LANES), lambda t, s: (0, 0))
    rowv = lambda width: pl.BlockSpec((1, width), lambda t, s: (0, 0))
    last = n_tiles - 1
    cm_idx = lambda t, s: (jnp.where(t > last, SUB_STEPS - 1, s), 0, jnp.minimum(t, last))
    nat_row = lambda t, s: jnp.where(t > 0, (t - 1) * SUB_STEPS + s, 0)
    out = pl.pallas_call(
        functools.partial(_out_kernel, n_tiles=n_tiles),
        grid=(n_tiles + 1, SUB_STEPS),
        in_specs=[pl.BlockSpec((SUB_TILE, D_MODEL), lambda t, s: (nat_row(t, s), 0)),
                  pl.BlockSpec((POS_PER_STEP, D_SSM, TILE_CHUNKS), cm_idx),
                  pl.BlockSpec((POS_PER_STEP, D_SSM, TILE_CHUNKS),
                               lambda t, s: (cm_idx(t, s)[0], 1, cm_idx(t, s)[2])),
                  pl.BlockSpec((SUB_TILE, D_ATTN), lambda t, s: (nat_row(t, s), 0)),
                  pl.BlockSpec((SUB_TILE, D_ATTN), lambda t, s: (nat_row(t, s), 3)),
                  pl.BlockSpec((D_SSM, D_SSM), lambda t, s: (0, 0)),
                  colv(D_SSM), colv(D_SSM), rowv(D_ATTN),
                  pl.BlockSpec((D_MODEL, D_MODEL), lambda t, s: (0, 0)),
                  rowv(D_MODEL)],
        out_specs=pl.BlockSpec((SUB_TILE, D_MODEL), lambda t, s: (nat_row(t, s), 0)),
        out_shape=jax.ShapeDtypeStruct((n, D_MODEL), F32),
        scratch_shapes=[pltpu.VMEM((2, D_SSM // LANES, TOKEN_TILE, LANES), F32)],
        compiler_params=_params(dimension_semantics=("arbitrary", "arbitrary")),
        name="out_proj",
    )(x2, y_cm, cm, o_attn, nat, w_glu_t, b_glu, gs, ga, w_out, gfin)
    return out.reshape(bsz, seq, D_MODEL)


def kernel(x_prompt, x_sample, norm_g, w_in, lam_re, lam_im, b_re, b_im, c_re, c_im, log_dt,
           d_skip, w_glu, b_glu, rpb, ssm_out_g, attn_out_g, w_out, final_norm_g):
    assert norm_g.shape[0] == 1, "single layer only"
    mt, wsi, wso, at = _s5_tables(lam_re[0], lam_im[0], b_re[0], b_im[0], c_re[0], c_im[0], log_dt[0])
    w = w_in[0].astype(BF16)
    w_cm = w[:, :D_CM].T
    w_nat = w[:, D_CM:]
    tabs = (norm_g[0][None], w_nat, w_cm, mt, wsi, wso, at, _bias_table(rpb[0]),
            _col(d_skip[0]).reshape(N_PAIRS, 2 * SSM_GROUP, LANES), w_glu[0].astype(BF16).T, _col(b_glu[0]), _col(ssm_out_g[0]),
            attn_out_g[0][None], w_out[0].astype(BF16), final_norm_g[None])
    return _trunk(x_prompt, tabs), _trunk(x_sample, tabs)
```

```python
import functools

import jax
import jax.numpy as jnp
import numpy as np
from jax import lax
from jax.experimental import pallas as pl
from jax.experimental.pallas import tpu as pltpu

F32 = jnp.float32
BF16 = jnp.bfloat16
HI = lax.Precision.HIGHEST

D_MODEL = 1024
D_SSM = 512
SSM_GROUP = 16
N_GROUPS = D_SSM // SSM_GROUP
N_PAIRS = N_GROUPS // 2
STATE_P = 64
N_HEADS = 8
HEAD_DIM = 64
D_ATTN = N_HEADS * HEAD_DIM
D_NAT = 3 * D_ATTN + D_ATTN
D_CM = 2 * D_SSM
GRID_W = 64
WIN_H = 8
WIN_W = 16
EPS = 1e-6
CHUNK = 16
PW = 2 * CHUNK * SSM_GROUP
NEG = -1e30

LANES = 128
TILE_CHUNKS = LANES
TOKEN_TILE = TILE_CHUNKS * CHUNK
SUB_STEPS = 4
SUB_TILE = TOKEN_TILE // SUB_STEPS
POS_PER_STEP = CHUNK // SUB_STEPS
IN_SUB_STEPS = 2
IN_SUB_TILE = TOKEN_TILE // IN_SUB_STEPS
IN_POS_PER_STEP = CHUNK // IN_SUB_STEPS
N_SLABS = D_MODEL // LANES
SCAN_PAIRS = 8
ATTN_ROWS_PER_STEP = 32
VMEM_LIMIT = 56 * 1024 * 1024


def _rms(x, g):
    return x * lax.rsqrt(jnp.mean(x * x, axis=-1, keepdims=True) + EPS) * g


_GELU_C0 = float(np.sqrt(2.0 / np.pi))
_GELU_C1 = 0.044715 * _GELU_C0


def _times_sigmoid(y, half_g):
    h = 0.5 * y
    return h + h * jnp.tanh(half_g)


def _silu(z):
    h = 0.5 * z
    return h + h * jnp.tanh(h)


def _gelu_tanh(x):
    h = 0.5 * x
    return h + h * jnp.tanh(x * (_GELU_C0 + _GELU_C1 * (x * x)))


def _s5_tables(lam_re, lam_im, b_re, b_im, c_re, c_im, log_dt):
    T, G, P, C = CHUNK, N_GROUPS, STATE_P, SSM_GROUP
    dt = jnp.exp(log_dt)[..., None]
    xr, xi = lam_re * dt, lam_im * dt
    n = jnp.arange(T + 1, dtype=F32)[:, None, None, None]
    mag = jnp.exp(n * xr)
    pr, pi = mag * jnp.cos(n * xi), mag * jnp.sin(n * xi)
    a_re, a_im = pr[1], pi[1]
    den = lam_re * lam_re + lam_im * lam_im
    co_re = ((a_re - 1.0) * lam_re + a_im * lam_im) / den
    co_im = (a_im * lam_re - (a_re - 1.0) * lam_im) / den
    bb_re = co_re[..., None] * b_re - co_im[..., None] * b_im
    bb_im = co_re[..., None] * b_im + co_im[..., None] * b_re

    def lanes_gp(w):
        w = w.reshape(2, N_PAIRS, 2, w.shape[2], P)
        return jnp.transpose(w, (0, 1, 3, 2, 4)).reshape(2, N_PAIRS, w.shape[3], 2 * P)

    pw_re = lanes_gp(jnp.transpose(pr, (1, 2, 0, 3)))
    pw_im = lanes_gp(jnp.transpose(pi, (1, 2, 0, 3)))
    bt_re = lanes_gp(jnp.transpose(bb_re, (0, 1, 3, 2)))
    bt_im = lanes_gp(jnp.transpose(bb_im, (0, 1, 3, 2)))
    ct_re, ct_im = lanes_gp(c_re), lanes_gp(c_im)

    small = lambda rows: pl.BlockSpec((2, None, rows, 2 * P), lambda p: (0, p, 0, 0))
    table = pl.BlockSpec((None, PW, PW), lambda p: (p, 0, 0))
    mt, wsi, wso = pl.pallas_call(
        _tables_kernel,
        grid=(N_PAIRS,),
        in_specs=[small(T + 1), small(T + 1), small(C), small(C), small(C), small(C)],
        out_specs=[table, table, table],
        out_shape=[jax.ShapeDtypeStruct((N_PAIRS, PW, PW), BF16)] * 3,
        name="s5_tables",
    )(pw_re, pw_im, bt_re, bt_im, ct_re, ct_im)

    at = jnp.stack([pr[T, 0], pi[T, 0], pr[T, 1], pi[T, 1]], axis=0)
    at = at.reshape(4, N_PAIRS, 2 * P).transpose(1, 0, 2).reshape(N_PAIRS, 8 * P)
    return mt, wsi, wso, at


def _tables_kernel(pwr_ref, pwi_ref, btr_ref, bti_ref, ctr_ref, cti_ref, mt_ref, wsi_ref, wso_ref):
    T, R = CHUNK, 2 * SSM_GROUP
    first = lax.broadcasted_iota(jnp.int32, (SSM_GROUP, 2 * STATE_P), 1) < STATE_P

    def rows32(ref, d):
        w = ref[d]
        zero = jnp.zeros_like(w)
        return jnp.concatenate([jnp.where(first, w, zero), jnp.where(first, zero, w)], axis=0)

    b_re = [rows32(btr_ref, d) for d in range(2)]
    b_im = [rows32(bti_ref, d) for d in range(2)]
    c_re = [rows32(ctr_ref, d) for d in range(2)]
    c_im = [rows32(cti_ref, d) for d in range(2)]
    pw = lambda d, n: (pwr_ref[d, n:n + 1, :], pwi_ref[d, n:n + 1, :])

    def cmul(w_re, w_im, d, n):
        ar, ai = pw(d, n)
        return w_re * ar - w_im * ai, w_re * ai + w_im * ar

    for j in range(T):
        f_re, f_im = cmul(b_re[0], b_im[0], 0, T - 1 - j)
        g_re, g_im = cmul(b_re[1], b_im[1], 1, j)
        for part, w in enumerate((f_re, f_im, g_re, g_im)):
            wsi_ref[j * R:(j + 1) * R, part * LANES:(part + 1) * LANES] = w.astype(BF16)

    for i in range(T):
        f_re, f_im = cmul(c_re[0], c_im[0], 0, i + 1)
        g_re, g_im = cmul(c_re[1], c_im[1], 1, T - i)
        for part, w in enumerate((f_re, -f_im, g_re, -g_im)):
            wso_ref[i * R:(i + 1) * R, part * LANES:(part + 1) * LANES] = w.astype(BF16)

    zero = jnp.zeros((R, 2 * STATE_P), F32)
    ca_f = [cmul(c_re[0], c_im[0], 0, n) for n in range(T)]
    ca_b = [cmul(c_re[1], c_im[1], 1, n) for n in range(T)]
    lag_rows = []
    for m in range(2 * T):
        f = ca_f[m - (T - 1)] if T - 1 <= m <= 2 * T - 2 else (zero, zero)
        b = ca_b[(T - 1) - m] if m <= T - 1 else (zero, zero)
        lag_rows.append(jnp.concatenate([f[0], f[1], b[0], b[1]], axis=1))
    ca_cat = jnp.concatenate(lag_rows, axis=0)
    b_cat = jnp.concatenate([b_re[0], -b_im[0], b_re[1], -b_im[1]], axis=1)
    kl = lax.dot_general(b_cat, ca_cat, (((1,), (1,)), ((), ())), precision=HI,
                         preferred_element_type=F32)
    mt_t = jnp.concatenate([kl[:, (T - 1 - j) * R:(T - 1 - j) * R + PW] for j in range(T)], axis=0)
    mt_ref[...] = mt_t.T.astype(BF16)


def _bias_table(rpb):
    qc = np.arange(GRID_W)[:, None]
    kc = np.arange(GRID_W)[None, :]
    q_start = np.clip(qc - WIN_W // 2, 0, GRID_W - WIN_W)
    valid = (kc >= q_start) & (kc < q_start + WIN_W)
    ci = np.clip(kc - qc, -(WIN_W - 1), WIN_W - 1) + (WIN_W - 1)
    onehot = (ci[None] == np.arange(2 * WIN_W - 1)[:, None, None]).astype(np.float32)
    t = jnp.einsum('hrc,cqk->hrqk', rpb.astype(F32), jnp.asarray(onehot), precision=HI)
    t = jnp.where(jnp.asarray(valid)[None, None], t, NEG)
    nrf = 2 * WIN_H - 2
    t = jnp.stack([t[:, :nrf], t[:, 1:nrf + 1]], axis=3)
    t = t.reshape(N_HEADS // 2, 2, nrf, GRID_W, 2 * GRID_W)
    return jnp.transpose(t, (0, 2, 1, 3, 4)).reshape(N_HEADS // 2, nrf, 2 * GRID_W, 2 * GRID_W)


def _col(v):
    return jnp.broadcast_to(v.astype(F32)[:, None], (v.shape[0], LANES))


def _in_proj_kernel(*refs):
    xs = refs[:N_SLABS]
    g_ref, wn_ref, wc_ref, nat_ref, cm_ref, hp_scr = refs[N_SLABS:]
    s = pl.program_id(1)
    g = g_ref[...]
    r0 = pl.multiple_of(s * IN_SUB_TILE, IN_SUB_TILE)
    xn = jnp.concatenate([x[pl.ds(r0, IN_SUB_TILE), :] for x in xs], axis=1)
    h = _rms(xn, g).astype(BF16)
    for jl in range(IN_POS_PER_STEP):
        j = s * IN_POS_PER_STEP + jl
        xj = jnp.concatenate([x[pl.ds(j, TILE_CHUNKS, stride=CHUNK), :] for x in xs], axis=1)
        hp_scr[jl * TILE_CHUNKS:(jl + 1) * TILE_CHUNKS, :] = _rms(xj, g).astype(BF16)
    nat_ref[...] = jnp.dot(h, wn_ref[...], preferred_element_type=F32).astype(BF16)
    cm = lax.dot_general(wc_ref[...], hp_scr[...], (((1,), (1,)), ((), ())), preferred_element_type=F32)
    for jl in range(IN_POS_PER_STEP):
        cm_ref[jl] = cm[:, jl * TILE_CHUNKS:(jl + 1) * TILE_CHUNKS].astype(BF16)


def _scan_kernel(u_ref, w_ref, a_ref, h_ref, sin, sout, *, cps):
    pitch = cps + 8
    pair_rows = 2 * SSM_GROUP
    for pr in range(SCAN_PAIRS):
        z = u_ref[:, pr * pair_rows:(pr + 1) * pair_rows, :].reshape(PW, cps)
        st = lax.dot_general(z, w_ref[pr], (((0,), (0,)), ((), ())), preferred_element_type=F32)
        for part in range(4):
            slab = part * SCAN_PAIRS + pr
            sin[pl.ds(slab * pitch, cps), :] = st[:, part * LANES:(part + 1) * LANES]

    ar_f, ai_f = a_ref[:, 0:128], a_ref[:, 128:256]
    ar_b, ai_b = a_ref[:, 256:384], a_ref[:, 384:512]
    rows = lambda part, k: pl.ds(part * SCAN_PAIRS * pitch + k, SCAN_PAIRS, stride=pitch)

    def body(k, carry):
        hfr, hfi, hbr, hbi = carry
        kb = cps - 1 - k
        sout[rows(0, k), :] = hfr
        sout[rows(1, k), :] = hfi
        sout[rows(2, kb), :] = hbr
        sout[rows(3, kb), :] = hbi
        nfr = ar_f * hfr - ai_f * hfi + sin[rows(0, k), :]
        nfi = ar_f * hfi + ai_f * hfr + sin[rows(1, k), :]
        nbr = ar_b * hbr - ai_b * hbi + sin[rows(2, kb), :]
        nbi = ar_b * hbi + ai_b * hbr + sin[rows(3, kb), :]
        return nfr, nfi, nbr, nbi

    z = jnp.zeros((SCAN_PAIRS, LANES), F32)
    lax.fori_loop(0, cps, body, (z, z, z, z))

    for pr in range(SCAN_PAIRS):
        for part in range(4):
            slab = part * SCAN_PAIRS + pr
            lane0 = (pr * 4 + part) * LANES
            h_ref[:, lane0:lane0 + LANES] = sout[pl.ds(slab * pitch, cps), :].astype(BF16)


def _s5_out_kernel(u_ref, h_ref, mt_ref, wso_ref, d_ref, y_ref):
    nc = u_ref.shape[-1]
    z = u_ref[...].reshape(PW, nc)
    y = jnp.dot(mt_ref[0], z, preferred_element_type=F32)
    y = y + lax.dot_general(wso_ref[0], h_ref[...], (((1,), (1,)), ((), ())),
                            preferred_element_type=F32)
    d = jnp.concatenate([d_ref[0]] * CHUNK, axis=0)
    y = _gelu_tanh(y + jnp.concatenate([d] * (nc // LANES), axis=1) * z.astype(F32))
    y_ref[...] = y.astype(BF16).reshape(CHUNK, 2 * SSM_GROUP, nc)


def _attn_kernel(q_ref, k_ref, v_ref, b_ref, o_ref, *, rows):
    lane = lax.broadcasted_iota(jnp.int32, (GRID_W, 2 * HEAD_DIM), 1)
    first = lane < HEAD_DIM
    nkeys = WIN_H * GRID_W
    ones = jnp.ones((nkeys, 2 * HEAD_DIM), BF16)

    def one_row(r):
        rs = jnp.clip(r - WIN_H // 2, 0, rows - WIN_H)
        ri0 = rs - r + (WIN_H - 1)
        q0 = pl.multiple_of(r * GRID_W, GRID_W)
        k0 = pl.multiple_of(rs * GRID_W, GRID_W)
        q = q_ref[pl.ds(q0, GRID_W), :] * jnp.asarray(HEAD_DIM ** -0.5, BF16)
        zero = jnp.zeros_like(q)
        q2 = jnp.concatenate([jnp.where(first, q, zero), jnp.where(first, zero, q)], axis=0)
        kw = k_ref[pl.ds(k0, nkeys), :]
        vw = jnp.concatenate([v_ref[pl.ds(k0, nkeys), :], ones], axis=1)
        s = lax.dot_general(q2, kw, (((1,), (1,)), ((), ())), preferred_element_type=F32)
        s = s + jnp.concatenate([b_ref[0, ri0 + 2 * m] for m in range(WIN_H // 2)], axis=1)
        p = jnp.exp(s - jnp.max(s, axis=-1, keepdims=True))
        ol = jnp.dot(p.astype(BF16), vw, preferred_element_type=F32)
        o2 = ol[:, :2 * HEAD_DIM] / ol[:, 2 * HEAD_DIM:]
        o = jnp.where(first, o2[:GRID_W], o2[GRID_W:])
        o_ref[pl.ds(q0, GRID_W), :] = o.astype(BF16)

    def body(rb, carry):
        for i in range(ATTN_ROWS_PER_STEP):
            one_row(rb * ATTN_ROWS_PER_STEP + i)
        return carry

    lax.fori_loop(0, rows // ATTN_ROWS_PER_STEP, body, 0)


def _out_kernel(x_ref, y_ref, zs_ref, o_ref, za_ref, wglu_ref, bglu_ref,
                gs_ref, ga_ref, wout_ref, gfin_ref, out_ref, ys_scr, *, n_tiles):
    t = pl.program_id(0)
    s = pl.program_id(1)
    two = lambda r: jnp.concatenate([r[...], r[...]], axis=1)

    @pl.when(t < n_tiles)
    def _():
        slot = t % 2
        for i0 in range(0, POS_PER_STEP, 2):
            cat = lambda ref: jnp.concatenate([ref[i0], ref[i0 + 1]], axis=1)
            yb = cat(y_ref)
            half_gate = jnp.dot(wglu_ref[...], yb, preferred_element_type=F32) + two(bglu_ref)
            y = _times_sigmoid(yb.astype(F32), half_gate)
            y = y * lax.rsqrt(jnp.mean(y * y, axis=0, keepdims=True) + EPS) * two(gs_ref)
            y = y * _silu(cat(zs_ref).astype(F32))
            for d in range(2):
                yt = y[:, d * LANES:(d + 1) * LANES].T
                pos = s * POS_PER_STEP + i0 + d
                for sl in range(D_SSM // LANES):
                    ys_scr[slot, sl, pl.ds(pos, TILE_CHUNKS, stride=CHUNK), :] = yt[:, sl * LANES:(sl + 1) * LANES]

    @pl.when(t > 0)
    def _():
        slot = (t + 1) % 2
        r0 = pl.multiple_of(s * SUB_TILE, SUB_TILE)
        ys = jnp.concatenate([ys_scr[slot, sl, pl.ds(r0, SUB_TILE), :] for sl in range(D_SSM // LANES)], axis=1)
        ya = _rms(o_ref[...].astype(F32), ga_ref[...]) * _silu(za_ref[...].astype(F32))
        mixed = jnp.concatenate([ys.astype(BF16), ya.astype(BF16)], axis=-1)
        out = x_ref[...] + jnp.dot(mixed, wout_ref[...], preferred_element_type=F32)
        out_ref[...] = _rms(out, gfin_ref[...])


def _params(**kw):
    return pltpu.CompilerParams(vmem_limit_bytes=VMEM_LIMIT, **kw)


def _trunk(x, tabs):
    (norm_g, w_nat, w_cm, mt, wsi, wso, at, bias, d_skip, w_glu_t, b_glu, gs, ga, w_out, gfin) = tabs
    bsz, seq, _ = x.shape
    n = bsz * seq
    nc = n // CHUNK
    cps = seq // CHUNK
    rows = seq // GRID_W
    x2 = x.reshape(n, D_MODEL)
    n_tiles = n // TOKEN_TILE

    nat, cm = pl.pallas_call(
        _in_proj_kernel,
        grid=(n_tiles, IN_SUB_STEPS),
        in_specs=[pl.BlockSpec((TOKEN_TILE, LANES), functools.partial(lambda sl, t, s: (t, sl), sl))
                  for sl in range(N_SLABS)]
                 + [pl.BlockSpec((1, D_MODEL), lambda t, s: (0, 0)),
                    pl.BlockSpec((D_MODEL, D_NAT), lambda t, s: (0, 0)),
                    pl.BlockSpec((D_CM, D_MODEL), lambda t, s: (0, 0))],
        out_specs=[pl.BlockSpec((IN_SUB_TILE, D_NAT), lambda t, s: (t * IN_SUB_STEPS + s, 0)),
                   pl.BlockSpec((IN_POS_PER_STEP, D_CM, TILE_CHUNKS), lambda t, s: (s, 0, t))],
        out_shape=[jax.ShapeDtypeStruct((n, D_NAT), BF16),
                   jax.ShapeDtypeStruct((CHUNK, D_CM, nc), BF16)],
        scratch_shapes=[pltpu.VMEM((IN_POS_PER_STEP * TILE_CHUNKS, D_MODEL), BF16)],
        compiler_params=_params(),
        name="in_proj",
    )(*([x2] * N_SLABS), norm_g, w_nat, w_cm)

    pair_rows = 2 * SSM_GROUP
    h_in = pl.pallas_call(
        functools.partial(_scan_kernel, cps=cps),
        grid=(N_PAIRS // SCAN_PAIRS, bsz),
        in_specs=[pl.BlockSpec((CHUNK, SCAN_PAIRS * pair_rows, cps), lambda m, b: (0, m, b)),
                  pl.BlockSpec((SCAN_PAIRS, PW, PW), lambda m, b: (m, 0, 0)),
                  pl.BlockSpec((SCAN_PAIRS, PW), lambda m, b: (m, 0))],
        out_specs=pl.BlockSpec((cps, SCAN_PAIRS * PW), lambda m, b: (b, m)),
        out_shape=jax.ShapeDtypeStruct((nc, N_PAIRS * PW), BF16),
        scratch_shapes=[pltpu.VMEM((4 * SCAN_PAIRS * (cps + 8), LANES), F32)] * 2,
        compiler_params=_params(),
        name="s5_scan",
    )(cm, wsi, at)

    y_cm = pl.pallas_call(
        _s5_out_kernel,
        grid=(N_PAIRS,),
        in_specs=[pl.BlockSpec((CHUNK, pair_rows, nc), lambda p: (0, p, 0)),
                  pl.BlockSpec((nc, PW), lambda p: (0, p)),
                  pl.BlockSpec((1, PW, PW), lambda p: (p, 0, 0)),
                  pl.BlockSpec((1, PW, PW), lambda p: (p, 0, 0)),
                  pl.BlockSpec((1, pair_rows, LANES), lambda p: (p, 0, 0))],
        out_specs=pl.BlockSpec((CHUNK, pair_rows, nc), lambda p: (0, p, 0)),
        out_shape=jax.ShapeDtypeStruct((CHUNK, D_SSM, nc), BF16),
        compiler_params=_params(),
        name="s5_out",
    )(cm, h_in, mt, wso, d_skip)

    nat2 = nat
    hp = 2 * HEAD_DIM
    o_attn = pl.pallas_call(
        functools.partial(_attn_kernel, rows=rows),
        grid=(N_HEADS // 2, bsz),
        in_specs=[pl.BlockSpec((seq, hp), lambda p, b: (b, p)),
                  pl.BlockSpec((seq, hp), lambda p, b: (b, 4 + p)),
                  pl.BlockSpec((seq, hp), lambda p, b: (b, 8 + p)),
                  pl.BlockSpec((1, 2 * WIN_H - 2, 2 * GRID_W, 2 * GRID_W), lambda p, b: (p, 0, 0, 0))],
        out_specs=pl.BlockSpec((seq, hp), lambda p, b: (b, p)),
        out_shape=jax.ShapeDtypeStruct((n, D_ATTN), BF16),
        compiler_params=_params(),
        name="attention",
    )(nat2, nat2, nat2, bias)

    colv = lambda width: pl.BlockSpec((width, LANES), lambda t, s: (0, 0))
    rowv = lambda width: pl.BlockSpec((1, width), lambda t, s: (0, 0))
    last = n_tiles - 1
    cm_idx = lambda t, s: (jnp.where(t > last, SUB_STEPS - 1, s), 0, jnp.minimum(t, last))
    nat_row = lambda t, s: jnp.where(t > 0, (t - 1) * SUB_STEPS + s, 0)
    out = pl.pallas_call(
        functools.partial(_out_kernel, n_tiles=n_tiles),
        grid=(n_tiles + 1, SUB_STEPS),
        in_specs=[pl.BlockSpec((SUB_TILE, D_MODEL), lambda t, s: (nat_row(t, s), 0)),
                  pl.BlockSpec((POS_PER_STEP, D_SSM, TILE_CHUNKS), cm_idx),
                  pl.BlockSpec((POS_PER_STEP, D_SSM, TILE_CHUNKS),
                               lambda t, s: (cm_idx(t, s)[0], 1, cm_idx(t, s)[2])),
                  pl.BlockSpec((SUB_TILE, D_ATTN), lambda t, s: (nat_row(t, s), 0)),
                  pl.BlockSpec((SUB_TILE, D_ATTN), lambda t, s: (nat_row(t, s), 3)),
                  pl.BlockSpec((D_SSM, D_SSM), lambda t, s: (0, 0)),
                  colv(D_SSM), colv(D_SSM), rowv(D_ATTN),
                  pl.BlockSpec((D_MODEL, D_MODEL), lambda t, s: (0, 0)),
                  rowv(D_MODEL)],
        out_specs=pl.BlockSpec((SUB_TILE, D_MODEL), lambda t, s: (nat_row(t, s), 0)),
        out_shape=jax.ShapeDtypeStruct((n, D_MODEL), F32),
        scratch_shapes=[pltpu.VMEM((2, D_SSM // LANES, TOKEN_TILE, LANES), F32)],
        compiler_params=_params(dimension_semantics=("arbitrary", "arbitrary")),
        name="out_proj",
    )(x2, y_cm, cm, o_attn, nat, w_glu_t, b_glu, gs, ga, w_out, gfin)
    return out.reshape(bsz, seq, D_MODEL)


def kernel(x_prompt, x_sample, norm_g, w_in, lam_re, lam_im, b_re, b_im, c_re, c_im, log_dt,
           d_skip, w_glu, b_glu, rpb, ssm_out_g, attn_out_g, w_out, final_norm_g):
    assert norm_g.shape[0] == 1, "single layer only"
    mt, wsi, wso, at = _s5_tables(lam_re[0], lam_im[0], b_re[0], b_im[0], c_re[0], c_im[0], log_dt[0])
    w = w_in[0].astype(BF16)
    w_cm = w[:, :D_CM].T
    w_nat = w[:, D_CM:]
    tabs = (norm_g[0][None], w_nat, w_cm, mt, wsi, wso, at, _bias_table(rpb[0]),
            _col(d_skip[0]).reshape(N_PAIRS, 2 * SSM_GROUP, LANES), (0.5 * w_glu[0]).astype(BF16).T, _col(0.5 * b_glu[0]), _col(ssm_out_g[0]),
            attn_out_g[0][None], w_out[0].astype(BF16), final_norm_g[None])
    return _trunk(x_prompt, tabs), _trunk(x_sample, tabs)
```

```python
import functools

import jax
import jax.numpy as jnp
import numpy as np
from jax import lax
from jax.experimental import pallas as pl
from jax.experimental.pallas import tpu as pltpu

F32 = jnp.float32
BF16 = jnp.bfloat16
HI = lax.Precision.HIGHEST

D_MODEL = 1024
D_SSM = 512
SSM_GROUP = 16
N_GROUPS = D_SSM // SSM_GROUP
N_PAIRS = N_GROUPS // 2
STATE_P = 64
N_HEADS = 8
HEAD_DIM = 64
D_ATTN = N_HEADS * HEAD_DIM
D_NAT = 3 * D_ATTN + D_ATTN
D_CM = 2 * D_SSM
GRID_W = 64
WIN_H = 8
WIN_W = 16
EPS = 1e-6
CHUNK = 16
PW = 2 * CHUNK * SSM_GROUP
NEG = -1e30

LANES = 128
TILE_CHUNKS = LANES
TOKEN_TILE = TILE_CHUNKS * CHUNK
SUB_STEPS = 4
SUB_TILE = TOKEN_TILE // SUB_STEPS
POS_PER_STEP = CHUNK // SUB_STEPS
IN_SUB_STEPS = 2
IN_SUB_TILE = TOKEN_TILE // IN_SUB_STEPS
IN_POS_PER_STEP = CHUNK // IN_SUB_STEPS
N_SLABS = D_MODEL // LANES
SCAN_PAIRS = 8
OUT_PAIRS = 2
ATTN_ROWS_PER_STEP = 32
VMEM_LIMIT = 56 * 1024 * 1024


def _rms(x, g):
    return x * lax.rsqrt(jnp.mean(x * x, axis=-1, keepdims=True) + EPS) * g


_GELU_C0 = float(np.sqrt(2.0 / np.pi))
_GELU_C1 = 0.044715 * _GELU_C0


def _times_sigmoid(y, half_g):
    h = 0.5 * y
    return h + h * jnp.tanh(half_g)


def _silu(z):
    h = 0.5 * z
    return h + h * jnp.tanh(h)


def _gelu_tanh(x):
    h = 0.5 * x
    return h + h * jnp.tanh(x * (_GELU_C0 + _GELU_C1 * (x * x)))


def _s5_tables(lam_re, lam_im, b_re, b_im, c_re, c_im, log_dt):
    T, G, P, C = CHUNK, N_GROUPS, STATE_P, SSM_GROUP
    dt = jnp.exp(log_dt)[..., None]
    xr, xi = lam_re * dt, lam_im * dt
    n = jnp.arange(T + 1, dtype=F32)[:, None, None, None]
    mag = jnp.exp(n * xr)
    pr, pi = mag * jnp.cos(n * xi), mag * jnp.sin(n * xi)
    a_re, a_im = pr[1], pi[1]
    den = lam_re * lam_re + lam_im * lam_im
    co_re = ((a_re - 1.0) * lam_re + a_im * lam_im) / den
    co_im = (a_im * lam_re - (a_re - 1.0) * lam_im) / den
    bb_re = co_re[..., None] * b_re - co_im[..., None] * b_im
    bb_im = co_re[..., None] * b_im + co_im[..., None] * b_re

    def lanes_gp(w):
        w = w.reshape(2, N_PAIRS, 2, w.shape[2], P)
        return jnp.transpose(w, (0, 1, 3, 2, 4)).reshape(2, N_PAIRS, w.shape[3], 2 * P)

    pw_re = lanes_gp(jnp.transpose(pr, (1, 2, 0, 3)))
    pw_im = lanes_gp(jnp.transpose(pi, (1, 2, 0, 3)))
    bt_re = lanes_gp(jnp.transpose(bb_re, (0, 1, 3, 2)))
    bt_im = lanes_gp(jnp.transpose(bb_im, (0, 1, 3, 2)))
    ct_re, ct_im = lanes_gp(c_re), lanes_gp(c_im)

    small = lambda rows: pl.BlockSpec((2, None, rows, 2 * P), lambda p: (0, p, 0, 0))
    table = pl.BlockSpec((None, PW, PW), lambda p: (p, 0, 0))
    mt, wsi, wso = pl.pallas_call(
        _tables_kernel,
        grid=(N_PAIRS,),
        in_specs=[small(T + 1), small(T + 1), small(C), small(C), small(C), small(C)],
        out_specs=[table, table, table],
        out_shape=[jax.ShapeDtypeStruct((N_PAIRS, PW, PW), BF16)] * 3,
        name="s5_tables",
    )(pw_re, pw_im, bt_re, bt_im, ct_re, ct_im)

    at = jnp.stack([pr[T, 0], pi[T, 0], pr[T, 1], pi[T, 1]], axis=0)
    at = at.reshape(4, N_PAIRS, 2 * P).transpose(1, 0, 2).reshape(N_PAIRS, 8 * P)
    return mt, wsi, wso, at


def _tables_kernel(pwr_ref, pwi_ref, btr_ref, bti_ref, ctr_ref, cti_ref, mt_ref, wsi_ref, wso_ref):
    T, R = CHUNK, 2 * SSM_GROUP
    first = lax.broadcasted_iota(jnp.int32, (SSM_GROUP, 2 * STATE_P), 1) < STATE_P

    def rows32(ref, d):
        w = ref[d]
        zero = jnp.zeros_like(w)
        return jnp.concatenate([jnp.where(first, w, zero), jnp.where(first, zero, w)], axis=0)

    b_re = [rows32(btr_ref, d) for d in range(2)]
    b_im = [rows32(bti_ref, d) for d in range(2)]
    c_re = [rows32(ctr_ref, d) for d in range(2)]
    c_im = [rows32(cti_ref, d) for d in range(2)]
    pw = lambda d, n: (pwr_ref[d, n:n + 1, :], pwi_ref[d, n:n + 1, :])

    def cmul(w_re, w_im, d, n):
        ar, ai = pw(d, n)
        return w_re * ar - w_im * ai, w_re * ai + w_im * ar

    for j in range(T):
        f_re, f_im = cmul(b_re[0], b_im[0], 0, T - 1 - j)
        g_re, g_im = cmul(b_re[1], b_im[1], 1, j)
        for part, w in enumerate((f_re, f_im, g_re, g_im)):
            wsi_ref[j * R:(j + 1) * R, part * LANES:(part + 1) * LANES] = w.astype(BF16)

    for i in range(T):
        f_re, f_im = cmul(c_re[0], c_im[0], 0, i + 1)
        g_re, g_im = cmul(c_re[1], c_im[1], 1, T - i)
        for part, w in enumerate((f_re, -f_im, g_re, -g_im)):
            wso_ref[i * R:(i + 1) * R, part * LANES:(part + 1) * LANES] = w.astype(BF16)

    zero = jnp.zeros((R, 2 * STATE_P), F32)
    ca_f = [cmul(c_re[0], c_im[0], 0, n) for n in range(T)]
    ca_b = [cmul(c_re[1], c_im[1], 1, n) for n in range(T)]
    lag_rows = []
    for m in range(2 * T):
        f = ca_f[m - (T - 1)] if T - 1 <= m <= 2 * T - 2 else (zero, zero)
        b = ca_b[(T - 1) - m] if m <= T - 1 else (zero, zero)
        lag_rows.append(jnp.concatenate([f[0], f[1], b[0], b[1]], axis=1))
    ca_cat = jnp.concatenate(lag_rows, axis=0)
    b_cat = jnp.concatenate([b_re[0], -b_im[0], b_re[1], -b_im[1]], axis=1)
    kl = lax.dot_general(b_cat, ca_cat, (((1,), (1,)), ((), ())), precision=HI,
                         preferred_element_type=F32)
    mt_t = jnp.concatenate([kl[:, (T - 1 - j) * R:(T - 1 - j) * R + PW] for j in range(T)], axis=0)
    mt_ref[...] = mt_t.T.astype(BF16)


def _bias_table(rpb):
    qc = np.arange(GRID_W)[:, None]
    kc = np.arange(GRID_W)[None, :]
    q_start = np.clip(qc - WIN_W // 2, 0, GRID_W - WIN_W)
    valid = (kc >= q_start) & (kc < q_start + WIN_W)
    ci = np.clip(kc - qc, -(WIN_W - 1), WIN_W - 1) + (WIN_W - 1)
    onehot = (ci[None] == np.arange(2 * WIN_W - 1)[:, None, None]).astype(np.float32)
    t = jnp.einsum('hrc,cqk->hrqk', rpb.astype(F32), jnp.asarray(onehot), precision=HI)
    t = jnp.where(jnp.asarray(valid)[None, None], t, NEG)
    nrf = 2 * WIN_H - 2
    t = jnp.stack([t[:, :nrf], t[:, 1:nrf + 1]], axis=3)
    t = t.reshape(N_HEADS // 2, 2, nrf, GRID_W, 2 * GRID_W)
    return jnp.transpose(t, (0, 2, 1, 3, 4)).reshape(N_HEADS // 2, nrf, 2 * GRID_W, 2 * GRID_W)


def _col(v):
    return jnp.broadcast_to(v.astype(F32)[:, None], (v.shape[0], LANES))


def _in_proj_kernel(*refs):
    xs = refs[:N_SLABS]
    g_ref, wn_ref, wc_ref, nat_ref, cm_ref, hp_scr = refs[N_SLABS:]
    s = pl.program_id(1)
    g = g_ref[...]
    r0 = pl.multiple_of(s * IN_SUB_TILE, IN_SUB_TILE)
    xn = jnp.concatenate([x[pl.ds(r0, IN_SUB_TILE), :] for x in xs], axis=1)
    h = _rms(xn, g).astype(BF16)
    for jl in range(IN_POS_PER_STEP):
        j = s * IN_POS_PER_STEP + jl
        xj = jnp.concatenate([x[pl.ds(j, TILE_CHUNKS, stride=CHUNK), :] for x in xs], axis=1)
        hp_scr[jl * TILE_CHUNKS:(jl + 1) * TILE_CHUNKS, :] = _rms(xj, g).astype(BF16)
    nat_ref[...] = jnp.dot(h, wn_ref[...], preferred_element_type=F32).astype(BF16)
    cm = lax.dot_general(wc_ref[...], hp_scr[...], (((1,), (1,)), ((), ())), preferred_element_type=F32)
    for jl in range(IN_POS_PER_STEP):
        cm_ref[jl] = cm[:, jl * TILE_CHUNKS:(jl + 1) * TILE_CHUNKS].astype(BF16)


def _scan_kernel(u_ref, w_ref, a_ref, h_ref, sin, sout, *, cps):
    pitch = cps + 8
    pair_rows = 2 * SSM_GROUP
    for pr in range(SCAN_PAIRS):
        z = u_ref[:, pr * pair_rows:(pr + 1) * pair_rows, :].reshape(PW, cps)
        st = lax.dot_general(z, w_ref[pr], (((0,), (0,)), ((), ())), preferred_element_type=F32)
        for part in range(4):
            slab = part * SCAN_PAIRS + pr
            sin[pl.ds(slab * pitch, cps), :] = st[:, part * LANES:(part + 1) * LANES]

    ar_f, ai_f = a_ref[:, 0:128], a_ref[:, 128:256]
    ar_b, ai_b = a_ref[:, 256:384], a_ref[:, 384:512]
    rows = lambda part, k: pl.ds(part * SCAN_PAIRS * pitch + k, SCAN_PAIRS, stride=pitch)

    def body(k, carry):
        hfr, hfi, hbr, hbi = carry
        kb = cps - 1 - k
        sout[rows(0, k), :] = hfr
        sout[rows(1, k), :] = hfi
        sout[rows(2, kb), :] = hbr
        sout[rows(3, kb), :] = hbi
        nfr = ar_f * hfr - ai_f * hfi + sin[rows(0, k), :]
        nfi = ar_f * hfi + ai_f * hfr + sin[rows(1, k), :]
        nbr = ar_b * hbr - ai_b * hbi + sin[rows(2, kb), :]
        nbi = ar_b * hbi + ai_b * hbr + sin[rows(3, kb), :]
        return nfr, nfi, nbr, nbi

    z = jnp.zeros((SCAN_PAIRS, LANES), F32)
    lax.fori_loop(0, cps, body, (z, z, z, z))

    for pr in range(SCAN_PAIRS):
        for part in range(4):
            slab = part * SCAN_PAIRS + pr
            lane0 = (pr * 4 + part) * LANES
            h_ref[:, lane0:lane0 + LANES] = sout[pl.ds(slab * pitch, cps), :].astype(BF16)


def _s5_out_kernel(u_ref, h_ref, mt_ref, wso_ref, d_ref, y_ref):
    nc = u_ref.shape[-1]
    pair_rows = 2 * SSM_GROUP
    nt = (((1,), (1,)), ((), ()))
    zs, ys = [], []
    for q in range(OUT_PAIRS):
        z = u_ref[:, q * pair_rows:(q + 1) * pair_rows, :].reshape(PW, nc)
        y = jnp.dot(mt_ref[q], z, preferred_element_type=F32)
        y = y + lax.dot_general(wso_ref[q], h_ref[:, q * PW:(q + 1) * PW], nt, preferred_element_type=F32)
        zs.append(z)
        ys.append(y)
    for q in range(OUT_PAIRS):
        d = jnp.concatenate([d_ref[q]] * CHUNK, axis=0)
        y = _gelu_tanh(ys[q] + jnp.concatenate([d] * (nc // LANES), axis=1) * zs[q].astype(F32))
        y_ref[:, q * pair_rows:(q + 1) * pair_rows, :] = y.astype(BF16).reshape(CHUNK, pair_rows, nc)


def _attn_kernel(q_ref, k_ref, v_ref, b_ref, o_ref, *, rows):
    lane = lax.broadcasted_iota(jnp.int32, (GRID_W, 2 * HEAD_DIM), 1)
    first = lane < HEAD_DIM
    nkeys = WIN_H * GRID_W
    ones = jnp.ones((nkeys, 2 * HEAD_DIM), BF16)

    def one_row(r):
        rs = jnp.clip(r - WIN_H // 2, 0, rows - WIN_H)
        ri0 = rs - r + (WIN_H - 1)
        q0 = pl.multiple_of(r * GRID_W, GRID_W)
        k0 = pl.multiple_of(rs * GRID_W, GRID_W)
        q = q_ref[pl.ds(q0, GRID_W), :] * jnp.asarray(HEAD_DIM ** -0.5, BF16)
        zero = jnp.zeros_like(q)
        q2 = jnp.concatenate([jnp.where(first, q, zero), jnp.where(first, zero, q)], axis=0)
        kw = k_ref[pl.ds(k0, nkeys), :]
        vw = jnp.concatenate([v_ref[pl.ds(k0, nkeys), :], ones], axis=1)
        s = lax.dot_general(q2, kw, (((1,), (1,)), ((), ())), preferred_element_type=F32)
        s = s + jnp.concatenate([b_ref[0, ri0 + 2 * m] for m in range(WIN_H // 2)], axis=1)
        p = jnp.exp(s - jnp.max(s, axis=-1, keepdims=True))
        ol = jnp.dot(p.astype(BF16), vw, preferred_element_type=F32)
        o2 = ol[:, :2 * HEAD_DIM] / ol[:, 2 * HEAD_DIM:]
        o = jnp.where(first, o2[:GRID_W], o2[GRID_W:])
        o_ref[pl.ds(q0, GRID_W), :] = o.astype(BF16)

    def body(rb, carry):
        for i in range(ATTN_ROWS_PER_STEP):
            one_row(rb * ATTN_ROWS_PER_STEP + i)
        return carry

    lax.fori_loop(0, rows // ATTN_ROWS_PER_STEP, body, 0)


def _out_kernel(x_ref, y_ref, zs_ref, o_ref, za_ref, wglu_ref, bglu_ref,
                gs_ref, ga_ref, wout_ref, gfin_ref, out_ref, ys_scr, *, n_tiles):
    t = pl.program_id(0)
    s = pl.program_id(1)
    two = lambda r: jnp.concatenate([r[...], r[...]], axis=1)

    @pl.when(t < n_tiles)
    def _():
        slot = t % 2
        for i0 in range(0, POS_PER_STEP, 2):
            cat = lambda ref: jnp.concatenate([ref[i0], ref[i0 + 1]], axis=1)
            yb = cat(y_ref)
            half_gate = jnp.dot(wglu_ref[...], yb, preferred_element_type=F32) + two(bglu_ref)
            y = _times_sigmoid(yb.astype(F32), half_gate)
            y = y * lax.rsqrt(jnp.mean(y * y, axis=0, keepdims=True) + EPS) * two(gs_ref)
            y = y * _silu(cat(zs_ref).astype(F32))
            for d in range(2):
                yt = y[:, d * LANES:(d + 1) * LANES].T
                pos = s * POS_PER_STEP + i0 + d
                for sl in range(D_SSM // LANES):
                    ys_scr[slot, sl, pl.ds(pos, TILE_CHUNKS, stride=CHUNK), :] = yt[:, sl * LANES:(sl + 1) * LANES]

    @pl.when(t > 0)
    def _():
        slot = (t + 1) % 2
        r0 = pl.multiple_of(s * SUB_TILE, SUB_TILE)
        ys = jnp.concatenate([ys_scr[slot, sl, pl.ds(r0, SUB_TILE), :] for sl in range(D_SSM // LANES)], axis=1)
        ya = _rms(o_ref[...].astype(F32), ga_ref[...]) * _silu(za_ref[...].astype(F32))
        mixed = jnp.concatenate([ys.astype(BF16), ya.astype(BF16)], axis=-1)
        out = x_ref[...] + jnp.dot(mixed, wout_ref[...], preferred_element_type=F32)
        out_ref[...] = _rms(out, gfin_ref[...])


def _params(**kw):
    return pltpu.CompilerParams(vmem_limit_bytes=VMEM_LIMIT, **kw)


def _trunk(x, tabs):
    (norm_g, w_nat, w_cm, mt, wsi, wso, at, bias, d_skip, w_glu_t, b_glu, gs, ga, w_out, gfin) = tabs
    bsz, seq, _ = x.shape
    n = bsz * seq
    nc = n // CHUNK
    cps = seq // CHUNK
    rows = seq // GRID_W
    x2 = x.reshape(n, D_MODEL)
    n_tiles = n // TOKEN_TILE

    nat, cm = pl.pallas_call(
        _in_proj_kernel,
        grid=(n_tiles, IN_SUB_STEPS),
        in_specs=[pl.BlockSpec((TOKEN_TILE, LANES), functools.partial(lambda sl, t, s: (t, sl), sl))
                  for sl in range(N_SLABS)]
                 + [pl.BlockSpec((1, D_MODEL), lambda t, s: (0, 0)),
                    pl.BlockSpec((D_MODEL, D_NAT), lambda t, s: (0, 0)),
                    pl.BlockSpec((D_CM, D_MODEL), lambda t, s: (0, 0))],
        out_specs=[pl.BlockSpec((IN_SUB_TILE, D_NAT), lambda t, s: (t * IN_SUB_STEPS + s, 0)),
                   pl.BlockSpec((IN_POS_PER_STEP, D_CM, TILE_CHUNKS), lambda t, s: (s, 0, t))],
        out_shape=[jax.ShapeDtypeStruct((n, D_NAT), BF16),
                   jax.ShapeDtypeStruct((CHUNK, D_CM, nc), BF16)],
        scratch_shapes=[pltpu.VMEM((IN_POS_PER_STEP * TILE_CHUNKS, D_MODEL), BF16)],
        compiler_params=_params(),
        name="in_proj",
    )(*([x2] * N_SLABS), norm_g, w_nat, w_cm)

    pair_rows = 2 * SSM_GROUP
    h_in = pl.pallas_call(
        functools.partial(_scan_kernel, cps=cps),
        grid=(N_PAIRS // SCAN_PAIRS, bsz),
        in_specs=[pl.BlockSpec((CHUNK, SCAN_PAIRS * pair_rows, cps), lambda m, b: (0, m, b)),
                  pl.BlockSpec((SCAN_PAIRS, PW, PW), lambda m, b: (m, 0, 0)),
                  pl.BlockSpec((SCAN_PAIRS, PW), lambda m, b: (m, 0))],
        out_specs=pl.BlockSpec((cps, SCAN_PAIRS * PW), lambda m, b: (b, m)),
        out_shape=jax.ShapeDtypeStruct((nc, N_PAIRS * PW), BF16),
        scratch_shapes=[pltpu.VMEM((4 * SCAN_PAIRS * (cps + 8), LANES), F32)] * 2,
        compiler_params=_params(),
        name="s5_scan",
    )(cm, wsi, at)

    y_cm = pl.pallas_call(
        _s5_out_kernel,
        grid=(N_PAIRS // OUT_PAIRS,),
        in_specs=[pl.BlockSpec((CHUNK, OUT_PAIRS * pair_rows, nc), lambda p: (0, p, 0)),
                  pl.BlockSpec((nc, OUT_PAIRS * PW), lambda p: (0, p)),
                  pl.BlockSpec((OUT_PAIRS, PW, PW), lambda p: (p, 0, 0)),
                  pl.BlockSpec((OUT_PAIRS, PW, PW), lambda p: (p, 0, 0)),
                  pl.BlockSpec((OUT_PAIRS, pair_rows, LANES), lambda p: (p, 0, 0))],
        out_specs=pl.BlockSpec((CHUNK, OUT_PAIRS * pair_rows, nc), lambda p: (0, p, 0)),
        out_shape=jax.ShapeDtypeStruct((CHUNK, D_SSM, nc), BF16),
        compiler_params=_params(),
        name="s5_out",
    )(cm, h_in, mt, wso, d_skip)

    hp = 2 * HEAD_DIM
    o_attn = pl.pallas_call(
        functools.partial(_attn_kernel, rows=rows),
        grid=(N_HEADS // 2, bsz),
        in_specs=[pl.BlockSpec((seq, hp), lambda p, b: (b, p)),
                  pl.BlockSpec((seq, hp), lambda p, b: (b, 4 + p)),
                  pl.BlockSpec((seq, hp), lambda p, b: (b, 8 + p)),
                  pl.BlockSpec((1, 2 * WIN_H - 2, 2 * GRID_W, 2 * GRID_W), lambda p, b: (p, 0, 0, 0))],
        out_specs=pl.BlockSpec((seq, hp), lambda p, b: (b, p)),
        out_shape=jax.ShapeDtypeStruct((n, D_ATTN), BF16),
        compiler_params=_params(),
        name="attention",
    )(nat, nat, nat, bias)

    colv = lambda width: pl.BlockSpec((width, LANES), lambda t, s: (0, 0))
    rowv = lambda width: pl.BlockSpec((1, width), lambda t, s: (0, 0))
    last = n_tiles - 1
    cm_idx = lambda t, s: (jnp.where(t > last, SUB_STEPS - 1, s), 0, jnp.minimum(t, last))
    nat_row = lambda t, s: jnp.where(t > 0, (t - 1) * SUB_STEPS + s, 0)
    out = pl.pallas_call(
        functools.partial(_out_kernel, n_tiles=n_tiles),
        grid=(n_tiles + 1, SUB_STEPS),
        in_specs=[pl.BlockSpec((SUB_TILE, D_MODEL), lambda t, s: (nat_row(t, s), 0)),
                  pl.BlockSpec((POS_PER_STEP, D_SSM, TILE_CHUNKS), cm_idx),
                  pl.BlockSpec((POS_PER_STEP, D_SSM, TILE_CHUNKS),
                               lambda t, s: (cm_idx(t, s)[0], 1, cm_idx(t, s)[2])),
                  pl.BlockSpec((SUB_TILE, D_ATTN), lambda t, s: (nat_row(t, s), 0)),
                  pl.BlockSpec((SUB_TILE, D_ATTN), lambda t, s: (nat_row(t, s), 3)),
                  pl.BlockSpec((D_SSM, D_SSM), lambda t, s: (0, 0)),
                  colv(D_SSM), colv(D_SSM), rowv(D_ATTN),
                  pl.BlockSpec((D_MODEL, D_MODEL), lambda t, s: (0, 0)),
                  rowv(D_MODEL)],
        out_specs=pl.BlockSpec((SUB_TILE, D_MODEL), lambda t, s: (nat_row(t, s), 0)),
        out_shape=jax.ShapeDtypeStruct((n, D_MODEL), F32),
        scratch_shapes=[pltpu.VMEM((2, D_SSM // LANES, TOKEN_TILE, LANES), F32)],
        compiler_params=_params(dimension_semantics=("arbitrary", "arbitrary")),
        name="out_proj",
    )(x2, y_cm, cm, o_attn, nat, w_glu_t, b_glu, gs, ga, w_out, gfin)
    return out.reshape(bsz, seq, D_MODEL)


def kernel(x_prompt, x_sample, norm_g, w_in, lam_re, lam_im, b_re, b_im, c_re, c_im, log_dt,
           d_skip, w_glu, b_glu, rpb, ssm_out_g, attn_out_g, w_out, final_norm_g):
    assert norm_g.shape[0] == 1, "single layer only"
    mt, wsi, wso, at = _s5_tables(lam_re[0], lam_im[0], b_re[0], b_im[0], c_re[0], c_im[0], log_dt[0])
    w = w_in[0].astype(BF16)
    w_cm = w[:, :D_CM].T
    w_nat = w[:, D_CM:]
    tabs = (norm_g[0][None], w_nat, w_cm, mt, wsi, wso, at, _bias_table(rpb[0]),
            _col(d_skip[0]).reshape(N_PAIRS, 2 * SSM_GROUP, LANES),
            (0.5 * w_glu[0]).astype(BF16).T, _col(0.5 * b_glu[0]), _col(ssm_out_g[0]),
            attn_out_g[0][None], w_out[0].astype(BF16), final_norm_g[None])
    return _trunk(x_prompt, tabs), _trunk(x_sample, tabs)
```

```python
import functools

import jax
import jax.numpy as jnp
import numpy as np
from jax import lax
from jax.experimental import pallas as pl
from jax.experimental.pallas import tpu as pltpu

F32 = jnp.float32
BF16 = jnp.bfloat16
HI = lax.Precision.HIGHEST

D_MODEL = 1024
D_SSM = 512
SSM_GROUP = 16
N_GROUPS = D_SSM // SSM_GROUP
N_PAIRS = N_GROUPS // 2
STATE_P = 64
N_HEADS = 8
HEAD_DIM = 64
D_ATTN = N_HEADS * HEAD_DIM
D_NAT = 3 * D_ATTN + D_ATTN
D_CM = 2 * D_SSM
GRID_W = 64
WIN_H = 8
WIN_W = 16
EPS = 1e-6
CHUNK = 16
PW = 2 * CHUNK * SSM_GROUP
NEG = -1e30

LANES = 128
TILE_CHUNKS = LANES
TOKEN_TILE = TILE_CHUNKS * CHUNK
SUB_STEPS = 4
SUB_TILE = TOKEN_TILE // SUB_STEPS
POS_PER_STEP = CHUNK // SUB_STEPS
IN_SUB_STEPS = 2
IN_SUB_TILE = TOKEN_TILE // IN_SUB_STEPS
IN_POS_PER_STEP = CHUNK // IN_SUB_STEPS
N_SLABS = D_MODEL // LANES
SCAN_PAIRS = 8
OUT_PAIRS = 2
ATTN_ROWS_PER_STEP = 32
VMEM_LIMIT = 56 * 1024 * 1024


def _rms(x, g):
    return x * lax.rsqrt(jnp.mean(x * x, axis=-1, keepdims=True) + EPS) * g


_GELU_C0 = float(np.sqrt(2.0 / np.pi))
_GELU_C1 = 0.044715 * _GELU_C0


def _times_sigmoid(y, half_g):
    h = 0.5 * y
    return h + h * jnp.tanh(half_g)


def _silu(z):
    h = 0.5 * z
    return h + h * jnp.tanh(h)


def _gelu_tanh(x):
    h = 0.5 * x
    return h + h * jnp.tanh(x * (_GELU_C0 + _GELU_C1 * (x * x)))


def _s5_tables(lam_re, lam_im, b_re, b_im, c_re, c_im, log_dt):
    T, G, P, C = CHUNK, N_GROUPS, STATE_P, SSM_GROUP
    dt = jnp.exp(log_dt)[..., None]
    xr, xi = lam_re * dt, lam_im * dt
    n = jnp.arange(T + 1, dtype=F32)[:, None, None, None]
    mag = jnp.exp(n * xr)
    pr, pi = mag * jnp.cos(n * xi), mag * jnp.sin(n * xi)
    a_re, a_im = pr[1], pi[1]
    den = lam_re * lam_re + lam_im * lam_im
    co_re = ((a_re - 1.0) * lam_re + a_im * lam_im) / den
    co_im = (a_im * lam_re - (a_re - 1.0) * lam_im) / den
    bb_re = co_re[..., None] * b_re - co_im[..., None] * b_im
    bb_im = co_re[..., None] * b_im + co_im[..., None] * b_re

    def lanes_gp(w):
        w = w.reshape(2, N_PAIRS, 2, w.shape[2], P)
        return jnp.transpose(w, (0, 1, 3, 2, 4)).reshape(2, N_PAIRS, w.shape[3], 2 * P)

    pw_re = lanes_gp(jnp.transpose(pr, (1, 2, 0, 3)))
    pw_im = lanes_gp(jnp.transpose(pi, (1, 2, 0, 3)))
    bt_re = lanes_gp(jnp.transpose(bb_re, (0, 1, 3, 2)))
    bt_im = lanes_gp(jnp.transpose(bb_im, (0, 1, 3, 2)))
    ct_re, ct_im = lanes_gp(c_re), lanes_gp(c_im)

    small = lambda rows: pl.BlockSpec((2, None, rows, 2 * P), lambda p: (0, p, 0, 0))
    table = pl.BlockSpec((None, PW, PW), lambda p: (p, 0, 0))
    mt, wsi, wso = pl.pallas_call(
        _tables_kernel,
        grid=(N_PAIRS,),
        in_specs=[small(T + 1), small(T + 1), small(C), small(C), small(C), small(C)],
        out_specs=[table, table, table],
        out_shape=[jax.ShapeDtypeStruct((N_PAIRS, PW, PW), BF16)] * 3,
        name="s5_tables",
    )(pw_re, pw_im, bt_re, bt_im, ct_re, ct_im)

    at = jnp.stack([pr[T, 0], pi[T, 0], pr[T, 1], pi[T, 1]], axis=0)
    at = at.reshape(4, N_PAIRS, 2 * P).transpose(1, 0, 2).reshape(N_PAIRS, 8 * P)
    return mt, wsi, wso, at


def _tables_kernel(pwr_ref, pwi_ref, btr_ref, bti_ref, ctr_ref, cti_ref, mt_ref, wsi_ref, wso_ref):
    T, R = CHUNK, 2 * SSM_GROUP
    first = lax.broadcasted_iota(jnp.int32, (SSM_GROUP, 2 * STATE_P), 1) < STATE_P

    def rows32(ref, d):
        w = ref[d]
        zero = jnp.zeros_like(w)
        return jnp.concatenate([jnp.where(first, w, zero), jnp.where(first, zero, w)], axis=0)

    b_re = [rows32(btr_ref, d) for d in range(2)]
    b_im = [rows32(bti_ref, d) for d in range(2)]
    c_re = [rows32(ctr_ref, d) for d in range(2)]
    c_im = [rows32(cti_ref, d) for d in range(2)]
    pw = lambda d, n: (pwr_ref[d, n:n + 1, :], pwi_ref[d, n:n + 1, :])

    def cmul(w_re, w_im, d, n):
        ar, ai = pw(d, n)
        return w_re * ar - w_im * ai, w_re * ai + w_im * ar

    for j in range(T):
        f_re, f_im = cmul(b_re[0], b_im[0], 0, T - 1 - j)
        g_re, g_im = cmul(b_re[1], b_im[1], 1, j)
        for part, w in enumerate((f_re, f_im, g_re, g_im)):
            wsi_ref[j * R:(j + 1) * R, part * LANES:(part + 1) * LANES] = w.astype(BF16)

    for i in range(T):
        f_re, f_im = cmul(c_re[0], c_im[0], 0, i + 1)
        g_re, g_im = cmul(c_re[1], c_im[1], 1, T - i)
        for part, w in enumerate((f_re, -f_im, g_re, -g_im)):
            wso_ref[i * R:(i + 1) * R, part * LANES:(part + 1) * LANES] = w.astype(BF16)

    zero = jnp.zeros((R, 2 * STATE_P), F32)
    ca_f = [cmul(c_re[0], c_im[0], 0, n) for n in range(T)]
    ca_b = [cmul(c_re[1], c_im[1], 1, n) for n in range(T)]
    lag_rows = []
    for m in range(2 * T):
        f = ca_f[m - (T - 1)] if T - 1 <= m <= 2 * T - 2 else (zero, zero)
        b = ca_b[(T - 1) - m] if m <= T - 1 else (zero, zero)
        lag_rows.append(jnp.concatenate([f[0], f[1], b[0], b[1]], axis=1))
    ca_cat = jnp.concatenate(lag_rows, axis=0)
    b_cat = jnp.concatenate([b_re[0], -b_im[0], b_re[1], -b_im[1]], axis=1)
    kl = lax.dot_general(b_cat, ca_cat, (((1,), (1,)), ((), ())), precision=HI,
                         preferred_element_type=F32)
    mt_t = jnp.concatenate([kl[:, (T - 1 - j) * R:(T - 1 - j) * R + PW] for j in range(T)], axis=0)
    mt_ref[...] = mt_t.T.astype(BF16)


def _bias_table(rpb):
    nrf = 2 * WIN_H - 2
    nr, nc_ = rpb.shape[1], rpb.shape[2]
    rpb_lanes = jnp.pad(rpb.astype(F32), ((0, 0), (0, 0), (0, LANES - nc_)))
    return pl.pallas_call(
        _bias_kernel,
        grid=(N_HEADS // 2,),
        in_specs=[pl.BlockSpec((2, nr, LANES), lambda p: (p, 0, 0))],
        out_specs=pl.BlockSpec((None, nrf, 2 * GRID_W, 2 * GRID_W), lambda p: (p, 0, 0, 0)),
        out_shape=jax.ShapeDtypeStruct((N_HEADS // 2, nrf, 2 * GRID_W, 2 * GRID_W), F32),
        name="attn_bias",
    )(rpb_lanes)


def _bias_kernel(rpb_ref, out_ref):
    qc = lax.broadcasted_iota(jnp.int32, (GRID_W, LANES), 0)
    kc = lax.broadcasted_iota(jnp.int32, (GRID_W, LANES), 1)
    q_start = jnp.clip(qc - WIN_W // 2, 0, GRID_W - WIN_W)
    valid = (kc >= q_start) & (kc < q_start + WIN_W)
    left = kc < GRID_W
    nrf = out_ref.shape[0]
    for hl in range(2):
        toe, toe_right = [], []
        for r in range(nrf + 1):
            row = jnp.broadcast_to(rpb_ref[hl, r:r + 1, :], (GRID_W, LANES))
            t = pltpu.roll(row, LANES - (WIN_W - 1), 1, stride=1, stride_axis=0)
            t = jnp.where(valid, t, NEG)
            toe.append(t)
            toe_right.append(pltpu.roll(t, GRID_W, 1))
        for rf in range(nrf):
            out_ref[rf, hl * GRID_W:(hl + 1) * GRID_W, :] = jnp.where(left, toe[rf], toe_right[rf + 1])


def _col(v):
    return jnp.broadcast_to(v.astype(F32)[:, None], (v.shape[0], LANES))


def _in_proj_kernel(*refs):
    xs = refs[:N_SLABS]
    g_ref, wn_ref, wc_ref, nat_ref, cm_ref, hp_scr = refs[N_SLABS:]
    s = pl.program_id(1)
    g = g_ref[...]
    r0 = pl.multiple_of(s * IN_SUB_TILE, IN_SUB_TILE)
    xn = jnp.concatenate([x[pl.ds(r0, IN_SUB_TILE), :] for x in xs], axis=1)
    h = _rms(xn, g).astype(BF16)
    for jl in range(IN_POS_PER_STEP):
        j = s * IN_POS_PER_STEP + jl
        xj = jnp.concatenate([x[pl.ds(j, TILE_CHUNKS, stride=CHUNK), :] for x in xs], axis=1)
        hp_scr[jl * TILE_CHUNKS:(jl + 1) * TILE_CHUNKS, :] = _rms(xj, g).astype(BF16)
    nat_ref[...] = jnp.dot(h, wn_ref[:, D_CM:], preferred_element_type=F32).astype(BF16)
    cm = lax.dot_general(wc_ref[...], hp_scr[...], (((1,), (1,)), ((), ())), preferred_element_type=F32)
    for jl in range(IN_POS_PER_STEP):
        cm_ref[jl] = cm[:, jl * TILE_CHUNKS:(jl + 1) * TILE_CHUNKS].astype(BF16)


def _scan_kernel(u_ref, w_ref, a_ref, h_ref, sin, sout, *, cps):
    pitch = cps + 8
    pair_rows = 2 * SSM_GROUP
    for pr in range(SCAN_PAIRS):
        z = u_ref[:, pr * pair_rows:(pr + 1) * pair_rows, :].reshape(PW, cps)
        st = lax.dot_general(z, w_ref[pr], (((0,), (0,)), ((), ())), preferred_element_type=F32)
        for part in range(4):
            slab = part * SCAN_PAIRS + pr
            sin[pl.ds(slab * pitch, cps), :] = st[:, part * LANES:(part + 1) * LANES]

    ar_f, ai_f = a_ref[:, 0:128], a_ref[:, 128:256]
    ar_b, ai_b = a_ref[:, 256:384], a_ref[:, 384:512]
    rows = lambda part, k: pl.ds(part * SCAN_PAIRS * pitch + k, SCAN_PAIRS, stride=pitch)

    def body(k, carry):
        hfr, hfi, hbr, hbi = carry
        kb = cps - 1 - k
        sout[rows(0, k), :] = hfr
        sout[rows(1, k), :] = hfi
        sout[rows(2, kb), :] = hbr
        sout[rows(3, kb), :] = hbi
        nfr = ar_f * hfr - ai_f * hfi + sin[rows(0, k), :]
        nfi = ar_f * hfi + ai_f * hfr + sin[rows(1, k), :]
        nbr = ar_b * hbr - ai_b * hbi + sin[rows(2, kb), :]
        nbi = ar_b * hbi + ai_b * hbr + sin[rows(3, kb), :]
        return nfr, nfi, nbr, nbi

    z = jnp.zeros((SCAN_PAIRS, LANES), F32)
    lax.fori_loop(0, cps, body, (z, z, z, z))

    for pr in range(SCAN_PAIRS):
        for part in range(4):
            slab = part * SCAN_PAIRS + pr
            lane0 = (pr * 4 + part) * LANES
            h_ref[:, lane0:lane0 + LANES] = sout[pl.ds(slab * pitch, cps), :].astype(BF16)


def _s5_out_kernel(u_ref, h_ref, mt_ref, wso_ref, d_ref, y_ref):
    nc = u_ref.shape[-1]
    pair_rows = 2 * SSM_GROUP
    nt = (((1,), (1,)), ((), ()))
    zs, ys = [], []
    for q in range(OUT_PAIRS):
        z = u_ref[:, q * pair_rows:(q + 1) * pair_rows, :].reshape(PW, nc)
        y = jnp.dot(mt_ref[q], z, preferred_element_type=F32)
        y = y + lax.dot_general(wso_ref[q], h_ref[:, q * PW:(q + 1) * PW], nt, preferred_element_type=F32)
        zs.append(z)
        ys.append(y)
    for q in range(OUT_PAIRS):
        d = jnp.concatenate([d_ref[q]] * CHUNK, axis=0)
        y = _gelu_tanh(ys[q] + jnp.concatenate([d] * (nc // LANES), axis=1) * zs[q].astype(F32))
        y_ref[:, q * pair_rows:(q + 1) * pair_rows, :] = y.astype(BF16).reshape(CHUNK, pair_rows, nc)


def _attn_kernel(q_ref, k_ref, v_ref, b_ref, o_ref, *, rows):
    lane = lax.broadcasted_iota(jnp.int32, (GRID_W, 2 * HEAD_DIM), 1)
    first = lane < HEAD_DIM
    nkeys = WIN_H * GRID_W
    ones = jnp.ones((nkeys, 2 * HEAD_DIM), BF16)

    def one_row(r):
        rs = jnp.clip(r - WIN_H // 2, 0, rows - WIN_H)
        ri0 = rs - r + (WIN_H - 1)
        q0 = pl.multiple_of(r * GRID_W, GRID_W)
        k0 = pl.multiple_of(rs * GRID_W, GRID_W)
        q = q_ref[pl.ds(q0, GRID_W), :] * jnp.asarray(HEAD_DIM ** -0.5, BF16)
        zero = jnp.zeros_like(q)
        q2 = jnp.concatenate([jnp.where(first, q, zero), jnp.where(first, zero, q)], axis=0)
        kw = k_ref[pl.ds(k0, nkeys), :]
        vw = jnp.concatenate([v_ref[pl.ds(k0, nkeys), :], ones], axis=1)
        s = lax.dot_general(q2, kw, (((1,), (1,)), ((), ())), preferred_element_type=F32)
        s = s + jnp.concatenate([b_ref[0, ri0 + 2 * m] for m in range(WIN_H // 2)], axis=1)
        p = jnp.exp(s - jnp.max(s, axis=-1, keepdims=True))
        ol = jnp.dot(p.astype(BF16), vw, preferred_element_type=F32)
        o2 = ol[:, :2 * HEAD_DIM] / ol[:, 2 * HEAD_DIM:]
        o = jnp.where(first, o2[:GRID_W], o2[GRID_W:])
        o_ref[pl.ds(q0, GRID_W), :] = o.astype(BF16)

    def body(rb, carry):
        for i in range(ATTN_ROWS_PER_STEP):
            one_row(rb * ATTN_ROWS_PER_STEP + i)
        return carry

    lax.fori_loop(0, rows // ATTN_ROWS_PER_STEP, body, 0)


def _out_kernel(x_ref, y_ref, zs_ref, o_ref, za_ref, wglu_ref, bglu_ref,
                gs_ref, ga_ref, wout_ref, gfin_ref, out_ref, ys_scr, *, n_tiles):
    t = pl.program_id(0)
    s = pl.program_id(1)
    two = lambda r: jnp.concatenate([r[...], r[...]], axis=1)

    @pl.when(t < n_tiles)
    def _():
        slot = t % 2
        for i0 in range(0, POS_PER_STEP, 2):
            cat = lambda ref: jnp.concatenate([ref[i0], ref[i0 + 1]], axis=1)
            yb = cat(y_ref)
            half_gate = jnp.dot(wglu_ref[...], yb, preferred_element_type=F32) + two(bglu_ref)
            y = _times_sigmoid(yb.astype(F32), half_gate)
            y = y * lax.rsqrt(jnp.mean(y * y, axis=0, keepdims=True) + EPS) * two(gs_ref)
            y = y * _silu(cat(zs_ref).astype(F32))
            for d in range(2):
                yt = y[:, d * LANES:(d + 1) * LANES].T
                pos = s * POS_PER_STEP + i0 + d
                for sl in range(D_SSM // LANES):
                    ys_scr[slot, sl, pl.ds(pos, TILE_CHUNKS, stride=CHUNK), :] = yt[:, sl * LANES:(sl + 1) * LANES]

    @pl.when(t > 0)
    def _():
        slot = (t + 1) % 2
        r0 = pl.multiple_of(s * SUB_TILE, SUB_TILE)
        ys = jnp.concatenate([ys_scr[slot, sl, pl.ds(r0, SUB_TILE), :] for sl in range(D_SSM // LANES)], axis=1)
        ya = _rms(o_ref[...].astype(F32), ga_ref[...]) * _silu(za_ref[...].astype(F32))
        mixed = jnp.concatenate([ys.astype(BF16), ya.astype(BF16)], axis=-1)
        out = x_ref[...] + jnp.dot(mixed, wout_ref[...], preferred_element_type=F32)
        out_ref[...] = _rms(out, gfin_ref[...])


def _params(**kw):
    return pltpu.CompilerParams(vmem_limit_bytes=VMEM_LIMIT, **kw)


def _trunk(x, tabs):
    (norm_g, w_all, w_cm, mt, wsi, wso, at, bias, d_skip, w_glu_t, b_glu, gs, ga, w_out, gfin) = tabs
    bsz, seq, _ = x.shape
    n = bsz * seq
    nc = n // CHUNK
    cps = seq // CHUNK
    rows = seq // GRID_W
    x2 = x.reshape(n, D_MODEL)
    n_tiles = n // TOKEN_TILE

    nat, cm = pl.pallas_call(
        _in_proj_kernel,
        grid=(n_tiles, IN_SUB_STEPS),
        in_specs=[pl.BlockSpec((TOKEN_TILE, LANES), functools.partial(lambda sl, t, s: (t, sl), sl))
                  for sl in range(N_SLABS)]
                 + [pl.BlockSpec((1, D_MODEL), lambda t, s: (0, 0)),
                    pl.BlockSpec((D_MODEL, D_CM + D_NAT), lambda t, s: (0, 0)),
                    pl.BlockSpec((D_CM, D_MODEL), lambda t, s: (0, 0))],
        out_specs=[pl.BlockSpec((IN_SUB_TILE, D_NAT), lambda t, s: (t * IN_SUB_STEPS + s, 0)),
                   pl.BlockSpec((IN_POS_PER_STEP, D_CM, TILE_CHUNKS), lambda t, s: (s, 0, t))],
        out_shape=[jax.ShapeDtypeStruct((n, D_NAT), BF16),
                   jax.ShapeDtypeStruct((CHUNK, D_CM, nc), BF16)],
        scratch_shapes=[pltpu.VMEM((IN_POS_PER_STEP * TILE_CHUNKS, D_MODEL), BF16)],
        compiler_params=_params(),
        name="in_proj",
    )(*([x2] * N_SLABS), norm_g, w_all, w_cm)

    pair_rows = 2 * SSM_GROUP
    h_in = pl.pallas_call(
        functools.partial(_scan_kernel, cps=cps),
        grid=(N_PAIRS // SCAN_PAIRS, bsz),
        in_specs=[pl.BlockSpec((CHUNK, SCAN_PAIRS * pair_rows, cps), lambda m, b: (0, m, b)),
                  pl.BlockSpec((SCAN_PAIRS, PW, PW), lambda m, b: (m, 0, 0)),
                  pl.BlockSpec((SCAN_PAIRS, PW), lambda m, b: (m, 0))],
        out_specs=pl.BlockSpec((cps, SCAN_PAIRS * PW), lambda m, b: (b, m)),
        out_shape=jax.ShapeDtypeStruct((nc, N_PAIRS * PW), BF16),
        scratch_shapes=[pltpu.VMEM((4 * SCAN_PAIRS * (cps + 8), LANES), F32)] * 2,
        compiler_params=_params(),
        name="s5_scan",
    )(cm, wsi, at)

    y_cm = pl.pallas_call(
        _s5_out_kernel,
        grid=(N_PAIRS // OUT_PAIRS,),
        in_specs=[pl.BlockSpec((CHUNK, OUT_PAIRS * pair_rows, nc), lambda p: (0, p, 0)),
                  pl.BlockSpec((nc, OUT_PAIRS * PW), lambda p: (0, p)),
                  pl.BlockSpec((OUT_PAIRS, PW, PW), lambda p: (p, 0, 0)),
                  pl.BlockSpec((OUT_PAIRS, PW, PW), lambda p: (p, 0, 0)),
                  pl.BlockSpec((OUT_PAIRS, pair_rows, LANES), lambda p: (p, 0, 0))],
        out_specs=pl.BlockSpec((CHUNK, OUT_PAIRS * pair_rows, nc), lambda p: (0, p, 0)),
        out_shape=jax.ShapeDtypeStruct((CHUNK, D_SSM, nc), BF16),
        compiler_params=_params(),
        name="s5_out",
    )(cm, h_in, mt, wso, d_skip)

    hp = 2 * HEAD_DIM
    o_attn = pl.pallas_call(
        functools.partial(_attn_kernel, rows=rows),
        grid=(N_HEADS // 2, bsz),
        in_specs=[pl.BlockSpec((seq, hp), lambda p, b: (b, p)),
                  pl.BlockSpec((seq, hp), lambda p, b: (b, 4 + p)),
                  pl.BlockSpec((seq, hp), lambda p, b: (b, 8 + p)),
                  pl.BlockSpec((1, 2 * WIN_H - 2, 2 * GRID_W, 2 * GRID_W), lambda p, b: (p, 0, 0, 0))],
        out_specs=pl.BlockSpec((seq, hp), lambda p, b: (b, p)),
        out_shape=jax.ShapeDtypeStruct((n, D_ATTN), BF16),
        compiler_params=_params(),
        name="attention",
    )(nat, nat, nat, bias)

    colv = lambda width: pl.BlockSpec((width, LANES), lambda t, s: (0, 0))
    rowv = lambda width: pl.BlockSpec((1, width), lambda t, s: (0, 0))
    last = n_tiles - 1
    cm_idx = lambda t, s: (jnp.where(t > last, SUB_STEPS - 1, s), 0, jnp.minimum(t, last))
    nat_row = lambda t, s: jnp.where(t > 0, (t - 1) * SUB_STEPS + s, 0)
    out = pl.pallas_call(
        functools.partial(_out_kernel, n_tiles=n_tiles),
        grid=(n_tiles + 1, SUB_STEPS),
        in_specs=[pl.BlockSpec((SUB_TILE, D_MODEL), lambda t, s: (nat_row(t, s), 0)),
                  pl.BlockSpec((POS_PER_STEP, D_SSM, TILE_CHUNKS), cm_idx),
                  pl.BlockSpec((POS_PER_STEP, D_SSM, TILE_CHUNKS),
                               lambda t, s: (cm_idx(t, s)[0], 1, cm_idx(t, s)[2])),
                  pl.BlockSpec((SUB_TILE, D_ATTN), lambda t, s: (nat_row(t, s), 0)),
                  pl.BlockSpec((SUB_TILE, D_ATTN), lambda t, s: (nat_row(t, s), 3)),
                  pl.BlockSpec((D_SSM, D_SSM), lambda t, s: (0, 0)),
                  colv(D_SSM), colv(D_SSM), rowv(D_ATTN),
                  pl.BlockSpec((D_MODEL, D_MODEL), lambda t, s: (0, 0)),
                  rowv(D_MODEL)],
        out_specs=pl.BlockSpec((SUB_TILE, D_MODEL), lambda t, s: (nat_row(t, s), 0)),
        out_shape=jax.ShapeDtypeStruct((n, D_MODEL), F32),
        scratch_shapes=[pltpu.VMEM((2, D_SSM // LANES, TOKEN_TILE, LANES), F32)],
        compiler_params=_params(dimension_semantics=("arbitrary", "arbitrary")),
        name="out_proj",
    )(x2, y_cm, cm, o_attn, nat, w_glu_t, b_glu, gs, ga, w_out, gfin)
    return out.reshape(bsz, seq, D_MODEL)


def kernel(x_prompt, x_sample, norm_g, w_in, lam_re, lam_im, b_re, b_im, c_re, c_im, log_dt,
           d_skip, w_glu, b_glu, rpb, ssm_out_g, attn_out_g, w_out, final_norm_g):
    assert norm_g.shape[0] == 1, "single layer only"
    mt, wsi, wso, at = _s5_tables(lam_re[0], lam_im[0], b_re[0], b_im[0], c_re[0], c_im[0], log_dt[0])
    w = w_in[0].astype(BF16)
    w_cm = w[:, :D_CM].T
    tabs = (norm_g[0][None], w, w_cm, mt, wsi, wso, at, _bias_table(rpb[0]),
            _col(d_skip[0]).reshape(N_PAIRS, 2 * SSM_GROUP, LANES),
            (0.5 * w_glu[0]).astype(BF16).T, _col(0.5 * b_glu[0]), _col(ssm_out_g[0]),
            attn_out_g[0][None], w_out[0].astype(BF16), final_norm_g[None])
    return _trunk(x_prompt, tabs), _trunk(x_sample, tabs)
```

```python
import functools

import jax
import jax.numpy as jnp
import numpy as np
from jax import lax
from jax.experimental import pallas as pl
from jax.experimental.pallas import tpu as pltpu

F32 = jnp.float32
BF16 = jnp.bfloat16
HI = lax.Precision.HIGHEST

D_MODEL = 1024
D_SSM = 512
SSM_GROUP = 16
N_GROUPS = D_SSM // SSM_GROUP
N_PAIRS = N_GROUPS // 2
STATE_P = 64
N_HEADS = 8
HEAD_DIM = 64
D_ATTN = N_HEADS * HEAD_DIM
D_NAT = 3 * D_ATTN + D_ATTN
D_CM = 2 * D_SSM
GRID_W = 64
WIN_H = 8
WIN_W = 16
EPS = 1e-6
CHUNK = 16
PW = 2 * CHUNK * SSM_GROUP
NEG = -1e30

LANES = 128
TILE_CHUNKS = LANES
TOKEN_TILE = TILE_CHUNKS * CHUNK
SUB_STEPS = 4
SUB_TILE = TOKEN_TILE // SUB_STEPS
POS_PER_STEP = CHUNK // SUB_STEPS
IN_SUB_STEPS = 2
IN_SUB_TILE = TOKEN_TILE // IN_SUB_STEPS
IN_POS_PER_STEP = CHUNK // IN_SUB_STEPS
N_SLABS = D_MODEL // LANES
SCAN_PAIRS = 8
OUT_PAIRS = 2
ATTN_ROWS_PER_STEP = 32
VMEM_LIMIT = 56 * 1024 * 1024


def _rms(x, g):
    return x * lax.rsqrt(jnp.mean(x * x, axis=-1, keepdims=True) + EPS) * g


_GELU_C0 = float(np.sqrt(2.0 / np.pi))
_GELU_C1 = 0.044715 * _GELU_C0


def _times_sigmoid(y, half_g):
    h = 0.5 * y
    return h + h * jnp.tanh(half_g)


def _silu(z):
    h = 0.5 * z
    return h + h * jnp.tanh(h)


def _gelu_tanh(x):
    h = 0.5 * x
    return h + h * jnp.tanh(x * (_GELU_C0 + _GELU_C1 * (x * x)))


def _s5_tables(lam_re, lam_im, b_re, b_im, c_re, c_im, log_dt):
    T, G, P, C = CHUNK, N_GROUPS, STATE_P, SSM_GROUP
    dt = jnp.exp(log_dt)[..., None]
    xr, xi = lam_re * dt, lam_im * dt
    n = jnp.arange(T + 1, dtype=F32)[:, None, None, None]
    mag = jnp.exp(n * xr)
    pr, pi = mag * jnp.cos(n * xi), mag * jnp.sin(n * xi)
    a_re, a_im = pr[1], pi[1]
    den = lam_re * lam_re + lam_im * lam_im
    co_re = ((a_re - 1.0) * lam_re + a_im * lam_im) / den
    co_im = (a_im * lam_re - (a_re - 1.0) * lam_im) / den
    bb_re = co_re[..., None] * b_re - co_im[..., None] * b_im
    bb_im = co_re[..., None] * b_im + co_im[..., None] * b_re

    def lanes_gp(w):
        w = w.reshape(2, N_PAIRS, 2, w.shape[2], P)
        return jnp.transpose(w, (0, 1, 3, 2, 4)).reshape(2, N_PAIRS, w.shape[3], 2 * P)

    pw_re = lanes_gp(jnp.transpose(pr, (1, 2, 0, 3)))
    pw_im = lanes_gp(jnp.transpose(pi, (1, 2, 0, 3)))
    bt_re = lanes_gp(jnp.transpose(bb_re, (0, 1, 3, 2)))
    bt_im = lanes_gp(jnp.transpose(bb_im, (0, 1, 3, 2)))
    ct_re, ct_im = lanes_gp(c_re), lanes_gp(c_im)

    small = lambda rows: pl.BlockSpec((2, None, rows, 2 * P), lambda p: (0, p, 0, 0))
    table = pl.BlockSpec((None, PW, PW), lambda p: (p, 0, 0))
    mt, wsi, wso = pl.pallas_call(
        _tables_kernel,
        grid=(N_PAIRS,),
        in_specs=[small(T + 1), small(T + 1), small(C), small(C), small(C), small(C)],
        out_specs=[table, table, table],
        out_shape=[jax.ShapeDtypeStruct((N_PAIRS, PW, PW), BF16)] * 3,
        name="s5_tables",
    )(pw_re, pw_im, bt_re, bt_im, ct_re, ct_im)

    at = jnp.stack([pr[T, 0], pi[T, 0], pr[T, 1], pi[T, 1]], axis=0)
    at = at.reshape(4, N_PAIRS, 2 * P).transpose(1, 0, 2).reshape(N_PAIRS, 8 * P)
    return mt, wsi, wso, at


def _tables_kernel(pwr_ref, pwi_ref, btr_ref, bti_ref, ctr_ref, cti_ref, mt_ref, wsi_ref, wso_ref):
    T, R = CHUNK, 2 * SSM_GROUP
    first = lax.broadcasted_iota(jnp.int32, (SSM_GROUP, 2 * STATE_P), 1) < STATE_P

    def rows32(ref, d):
        w = ref[d]
        zero = jnp.zeros_like(w)
        return jnp.concatenate([jnp.where(first, w, zero), jnp.where(first, zero, w)], axis=0)

    b_re = [rows32(btr_ref, d) for d in range(2)]
    b_im = [rows32(bti_ref, d) for d in range(2)]
    c_re = [rows32(ctr_ref, d) for d in range(2)]
    c_im = [rows32(cti_ref, d) for d in range(2)]
    pw = lambda d, n: (pwr_ref[d, n:n + 1, :], pwi_ref[d, n:n + 1, :])

    def cmul(w_re, w_im, d, n):
        ar, ai = pw(d, n)
        return w_re * ar - w_im * ai, w_re * ai + w_im * ar

    for j in range(T):
        f_re, f_im = cmul(b_re[0], b_im[0], 0, T - 1 - j)
        g_re, g_im = cmul(b_re[1], b_im[1], 1, j)
        for part, w in enumerate((f_re, f_im, g_re, g_im)):
            wsi_ref[j * R:(j + 1) * R, part * LANES:(part + 1) * LANES] = w.astype(BF16)

    for i in range(T):
        f_re, f_im = cmul(c_re[0], c_im[0], 0, i + 1)
        g_re, g_im = cmul(c_re[1], c_im[1], 1, T - i)
        for part, w in enumerate((f_re, -f_im, g_re, -g_im)):
            wso_ref[i * R:(i + 1) * R, part * LANES:(part + 1) * LANES] = w.astype(BF16)

    zero = jnp.zeros((R, 2 * STATE_P), F32)
    ca_f = [cmul(c_re[0], c_im[0], 0, n) for n in range(T)]
    ca_b = [cmul(c_re[1], c_im[1], 1, n) for n in range(T)]
    lag_rows = []
    for m in range(2 * T):
        f = ca_f[m - (T - 1)] if T - 1 <= m <= 2 * T - 2 else (zero, zero)
        b = ca_b[(T - 1) - m] if m <= T - 1 else (zero, zero)
        lag_rows.append(jnp.concatenate([f[0], f[1], b[0], b[1]], axis=1))
    ca_cat = jnp.concatenate(lag_rows, axis=0)
    b_cat = jnp.concatenate([b_re[0], -b_im[0], b_re[1], -b_im[1]], axis=1)
    kl = lax.dot_general(b_cat, ca_cat, (((1,), (1,)), ((), ())), precision=HI,
                         preferred_element_type=F32)
    mt_t = jnp.concatenate([kl[:, (T - 1 - j) * R:(T - 1 - j) * R + PW] for j in range(T)], axis=0)
    mt_ref[...] = mt_t.T.astype(BF16)


def _bias_table(rpb):
    nrf = 2 * WIN_H - 2
    nr, nc_ = rpb.shape[1], rpb.shape[2]
    rpb_lanes = jnp.pad(rpb.astype(F32), ((0, 0), (0, 0), (0, LANES - nc_)))
    return pl.pallas_call(
        _bias_kernel,
        grid=(N_HEADS // 2,),
        in_specs=[pl.BlockSpec((2, nr, LANES), lambda p: (p, 0, 0))],
        out_specs=pl.BlockSpec((None, nrf, 2 * GRID_W, 2 * GRID_W), lambda p: (p, 0, 0, 0)),
        out_shape=jax.ShapeDtypeStruct((N_HEADS // 2, nrf, 2 * GRID_W, 2 * GRID_W), F32),
        name="attn_bias",
    )(rpb_lanes)


def _bias_kernel(rpb_ref, out_ref):
    qc = lax.broadcasted_iota(jnp.int32, (GRID_W, LANES), 0)
    kc = lax.broadcasted_iota(jnp.int32, (GRID_W, LANES), 1)
    q_start = jnp.clip(qc - WIN_W // 2, 0, GRID_W - WIN_W)
    valid = (kc >= q_start) & (kc < q_start + WIN_W)
    left = kc < GRID_W
    nrf = out_ref.shape[0]
    for hl in range(2):
        toe, toe_right = [], []
        for r in range(nrf + 1):
            row = jnp.broadcast_to(rpb_ref[hl, r:r + 1, :], (GRID_W, LANES))
            t = pltpu.roll(row, LANES - (WIN_W - 1), 1, stride=1, stride_axis=0)
            t = jnp.where(valid, t, NEG)
            toe.append(t)
            toe_right.append(pltpu.roll(t, GRID_W, 1))
        for rf in range(nrf):
            out_ref[rf, hl * GRID_W:(hl + 1) * GRID_W, :] = jnp.where(left, toe[rf], toe_right[rf + 1])


def _col(v):
    return jnp.broadcast_to(v.astype(F32)[:, None], (v.shape[0], LANES))


def _in_proj_kernel(*refs):
    xs = refs[:N_SLABS]
    g_ref, w_ref, nat_ref, cm_ref, hp_scr = refs[N_SLABS:]
    s = pl.program_id(1)
    g = g_ref[...]
    r0 = pl.multiple_of(s * IN_SUB_TILE, IN_SUB_TILE)
    xn = jnp.concatenate([x[pl.ds(r0, IN_SUB_TILE), :] for x in xs], axis=1)
    h = _rms(xn, g).astype(BF16)
    for jl in range(IN_POS_PER_STEP):
        j = s * IN_POS_PER_STEP + jl
        xj = jnp.concatenate([x[pl.ds(j, TILE_CHUNKS, stride=CHUNK), :] for x in xs], axis=1)
        hp_scr[jl * TILE_CHUNKS:(jl + 1) * TILE_CHUNKS, :] = _rms(xj, g).astype(BF16)
    nat_ref[...] = jnp.dot(h, w_ref[:, D_CM:], preferred_element_type=F32).astype(BF16)
    cm = lax.dot_general(w_ref[:, :D_CM], hp_scr[...], (((0,), (1,)), ((), ())), preferred_element_type=F32)
    for jl in range(IN_POS_PER_STEP):
        cm_ref[jl] = cm[:, jl * TILE_CHUNKS:(jl + 1) * TILE_CHUNKS].astype(BF16)


def _scan_kernel(u_ref, w_ref, a_ref, h_ref, sin, sout, *, cps):
    pitch = cps + 8
    pair_rows = 2 * SSM_GROUP
    for pr in range(SCAN_PAIRS):
        z = u_ref[:, pr * pair_rows:(pr + 1) * pair_rows, :].reshape(PW, cps)
        st = lax.dot_general(z, w_ref[pr], (((0,), (0,)), ((), ())), preferred_element_type=F32)
        for part in range(4):
            slab = part * SCAN_PAIRS + pr
            sin[pl.ds(slab * pitch, cps), :] = st[:, part * LANES:(part + 1) * LANES]

    ar_f, ai_f = a_ref[:, 0:128], a_ref[:, 128:256]
    ar_b, ai_b = a_ref[:, 256:384], a_ref[:, 384:512]
    rows = lambda part, k: pl.ds(part * SCAN_PAIRS * pitch + k, SCAN_PAIRS, stride=pitch)

    def body(k, carry):
        hfr, hfi, hbr, hbi = carry
        kb = cps - 1 - k
        sout[rows(0, k), :] = hfr
        sout[rows(1, k), :] = hfi
        sout[rows(2, kb), :] = hbr
        sout[rows(3, kb), :] = hbi
        nfr = ar_f * hfr - ai_f * hfi + sin[rows(0, k), :]
        nfi = ar_f * hfi + ai_f * hfr + sin[rows(1, k), :]
        nbr = ar_b * hbr - ai_b * hbi + sin[rows(2, kb), :]
        nbi = ar_b * hbi + ai_b * hbr + sin[rows(3, kb), :]
        return nfr, nfi, nbr, nbi

    z = jnp.zeros((SCAN_PAIRS, LANES), F32)
    lax.fori_loop(0, cps, body, (z, z, z, z))

    for pr in range(SCAN_PAIRS):
        for part in range(4):
            slab = part * SCAN_PAIRS + pr
            lane0 = (pr * 4 + part) * LANES
            h_ref[:, lane0:lane0 + LANES] = sout[pl.ds(slab * pitch, cps), :].astype(BF16)


def _s5_out_kernel(u_ref, h_ref, mt_ref, wso_ref, d_ref, y_ref):
    nc = u_ref.shape[-1]
    pair_rows = 2 * SSM_GROUP
    nt = (((1,), (1,)), ((), ()))
    zs, ys = [], []
    for q in range(OUT_PAIRS):
        z = u_ref[:, q * pair_rows:(q + 1) * pair_rows, :].reshape(PW, nc)
        y = jnp.dot(mt_ref[q], z, preferred_element_type=F32)
        y = y + lax.dot_general(wso_ref[q], h_ref[:, q * PW:(q + 1) * PW], nt, preferred_element_type=F32)
        zs.append(z)
        ys.append(y)
    for q in range(OUT_PAIRS):
        d = jnp.concatenate([d_ref[q]] * CHUNK, axis=0)
        y = _gelu_tanh(ys[q] + jnp.concatenate([d] * (nc // LANES), axis=1) * zs[q].astype(F32))
        y_ref[:, q * pair_rows:(q + 1) * pair_rows, :] = y.astype(BF16).reshape(CHUNK, pair_rows, nc)


def _attn_kernel(q_ref, k_ref, v_ref, b_ref, o_ref, *, rows):
    lane = lax.broadcasted_iota(jnp.int32, (GRID_W, 2 * HEAD_DIM), 1)
    first = lane < HEAD_DIM
    nkeys = WIN_H * GRID_W
    ones = jnp.ones((nkeys, 2 * HEAD_DIM), BF16)

    def one_row(r):
        rs = jnp.clip(r - WIN_H // 2, 0, rows - WIN_H)
        ri0 = rs - r + (WIN_H - 1)
        q0 = pl.multiple_of(r * GRID_W, GRID_W)
        k0 = pl.multiple_of(rs * GRID_W, GRID_W)
        q = q_ref[pl.ds(q0, GRID_W), :] * jnp.asarray(HEAD_DIM ** -0.5, BF16)
        zero = jnp.zeros_like(q)
        q2 = jnp.concatenate([jnp.where(first, q, zero), jnp.where(first, zero, q)], axis=0)
        kw = k_ref[pl.ds(k0, nkeys), :]
        vw = jnp.concatenate([v_ref[pl.ds(k0, nkeys), :], ones], axis=1)
        s = lax.dot_general(q2, kw, (((1,), (1,)), ((), ())), preferred_element_type=F32)
        s = s + jnp.concatenate([b_ref[0, ri0 + 2 * m] for m in range(WIN_H // 2)], axis=1)
        p = jnp.exp(s - jnp.max(s, axis=-1, keepdims=True))
        ol = jnp.dot(p.astype(BF16), vw, preferred_element_type=F32)
        o2 = ol[:, :2 * HEAD_DIM] / ol[:, 2 * HEAD_DIM:]
        o = jnp.where(first, o2[:GRID_W], o2[GRID_W:])
        o_ref[pl.ds(q0, GRID_W), :] = o.astype(BF16)

    def body(rb, carry):
        for i in range(ATTN_ROWS_PER_STEP):
            one_row(rb * ATTN_ROWS_PER_STEP + i)
        return carry

    lax.fori_loop(0, rows // ATTN_ROWS_PER_STEP, body, 0)


def _out_kernel(x_ref, y_ref, zs_ref, o_ref, za_ref, wglu_ref, bglu_ref,
                gs_ref, ga_ref, wout_ref, gfin_ref, out_ref, ys_scr, *, n_tiles):
    t = pl.program_id(0)
    s = pl.program_id(1)
    two = lambda r: jnp.concatenate([r[...], r[...]], axis=1)

    @pl.when(t < n_tiles)
    def _():
        slot = t % 2
        for i0 in range(0, POS_PER_STEP, 2):
            cat = lambda ref: jnp.concatenate([ref[i0], ref[i0 + 1]], axis=1)
            yb = cat(y_ref)
            half_gate = jnp.dot(wglu_ref[...], yb, preferred_element_type=F32) + two(bglu_ref)
            y = _times_sigmoid(yb.astype(F32), half_gate)
            y = y * lax.rsqrt(jnp.mean(y * y, axis=0, keepdims=True) + EPS) * two(gs_ref)
            y = y * _silu(cat(zs_ref).astype(F32))
            for d in range(2):
                yt = y[:, d * LANES:(d + 1) * LANES].T
                pos = s * POS_PER_STEP + i0 + d
                for sl in range(D_SSM // LANES):
                    ys_scr[slot, sl, pl.ds(pos, TILE_CHUNKS, stride=CHUNK), :] = yt[:, sl * LANES:(sl + 1) * LANES]

    @pl.when(t > 0)
    def _():
        slot = (t + 1) % 2
        r0 = pl.multiple_of(s * SUB_TILE, SUB_TILE)
        ys = jnp.concatenate([ys_scr[slot, sl, pl.ds(r0, SUB_TILE), :] for sl in range(D_SSM // LANES)], axis=1)
        ya = _rms(o_ref[...].astype(F32), ga_ref[...]) * _silu(za_ref[...].astype(F32))
        mixed = jnp.concatenate([ys.astype(BF16), ya.astype(BF16)], axis=-1)
        out = x_ref[...] + jnp.dot(mixed, wout_ref[...], preferred_element_type=F32)
        out_ref[...] = _rms(out, gfin_ref[...])


def _params(**kw):
    return pltpu.CompilerParams(vmem_limit_bytes=VMEM_LIMIT, **kw)


def _trunk(x, tabs):
    (norm_g, w_all, mt, wsi, wso, at, bias, d_skip, w_glu_t, b_glu, gs, ga, w_out, gfin) = tabs
    bsz, seq, _ = x.shape
    n = bsz * seq
    nc = n // CHUNK
    cps = seq // CHUNK
    rows = seq // GRID_W
    x2 = x.reshape(n, D_MODEL)
    n_tiles = n // TOKEN_TILE

    nat, cm = pl.pallas_call(
        _in_proj_kernel,
        grid=(n_tiles, IN_SUB_STEPS),
        in_specs=[pl.BlockSpec((TOKEN_TILE, LANES), functools.partial(lambda sl, t, s: (t, sl), sl))
                  for sl in range(N_SLABS)]
                 + [pl.BlockSpec((1, D_MODEL), lambda t, s: (0, 0)),
                    pl.BlockSpec((D_MODEL, D_CM + D_NAT), lambda t, s: (0, 0))],
        out_specs=[pl.BlockSpec((IN_SUB_TILE, D_NAT), lambda t, s: (t * IN_SUB_STEPS + s, 0)),
                   pl.BlockSpec((IN_POS_PER_STEP, D_CM, TILE_CHUNKS), lambda t, s: (s, 0, t))],
        out_shape=[jax.ShapeDtypeStruct((n, D_NAT), BF16),
                   jax.ShapeDtypeStruct((CHUNK, D_CM, nc), BF16)],
        scratch_shapes=[pltpu.VMEM((IN_POS_PER_STEP * TILE_CHUNKS, D_MODEL), BF16)],
        compiler_params=_params(),
        name="in_proj",
    )(*([x2] * N_SLABS), norm_g, w_all)

    pair_rows = 2 * SSM_GROUP
    h_in = pl.pallas_call(
        functools.partial(_scan_kernel, cps=cps),
        grid=(N_PAIRS // SCAN_PAIRS, bsz),
        in_specs=[pl.BlockSpec((CHUNK, SCAN_PAIRS * pair_rows, cps), lambda m, b: (0, m, b)),
                  pl.BlockSpec((SCAN_PAIRS, PW, PW), lambda m, b: (m, 0, 0)),
                  pl.BlockSpec((SCAN_PAIRS, PW), lambda m, b: (m, 0))],
        out_specs=pl.BlockSpec((cps, SCAN_PAIRS * PW), lambda m, b: (b, m)),
        out_shape=jax.ShapeDtypeStruct((nc, N_PAIRS * PW), BF16),
        scratch_shapes=[pltpu.VMEM((4 * SCAN_PAIRS * (cps + 8), LANES), F32)] * 2,
        compiler_params=_params(),
        name="s5_scan",
    )(cm, wsi, at)

    y_cm = pl.pallas_call(
        _s5_out_kernel,
        grid=(N_PAIRS // OUT_PAIRS,),
        in_specs=[pl.BlockSpec((CHUNK, OUT_PAIRS * pair_rows, nc), lambda p: (0, p, 0)),
                  pl.BlockSpec((nc, OUT_PAIRS * PW), lambda p: (0, p)),
                  pl.BlockSpec((OUT_PAIRS, PW, PW), lambda p: (p, 0, 0)),
                  pl.BlockSpec((OUT_PAIRS, PW, PW), lambda p: (p, 0, 0)),
                  pl.BlockSpec((OUT_PAIRS, pair_rows, LANES), lambda p: (p, 0, 0))],
        out_specs=pl.BlockSpec((CHUNK, OUT_PAIRS * pair_rows, nc), lambda p: (0, p, 0)),
        out_shape=jax.ShapeDtypeStruct((CHUNK, D_SSM, nc), BF16),
        compiler_params=_params(),
        name="s5_out",
    )(cm, h_in, mt, wso, d_skip)

    hp = 2 * HEAD_DIM
    o_attn = pl.pallas_call(
        functools.partial(_attn_kernel, rows=rows),
        grid=(N_HEADS // 2, bsz),
        in_specs=[pl.BlockSpec((seq, hp), lambda p, b: (b, p)),
                  pl.BlockSpec((seq, hp), lambda p, b: (b, 4 + p)),
                  pl.BlockSpec((seq, hp), lambda p, b: (b, 8 + p)),
                  pl.BlockSpec((1, 2 * WIN_H - 2, 2 * GRID_W, 2 * GRID_W), lambda p, b: (p, 0, 0, 0))],
        out_specs=pl.BlockSpec((seq, hp), lambda p, b: (b, p)),
        out_shape=jax.ShapeDtypeStruct((n, D_ATTN), BF16),
        compiler_params=_params(),
        name="attention",
    )(nat, nat, nat, bias)

    colv = lambda width: pl.BlockSpec((width, LANES), lambda t, s: (0, 0))
    rowv = lambda width: pl.BlockSpec((1, width), lambda t, s: (0, 0))
    last = n_tiles - 1
    cm_idx = lambda t, s: (jnp.where(t > last, SUB_STEPS - 1, s), 0, jnp.minimum(t, last))
    nat_row = lambda t, s: jnp.where(t > 0, (t - 1) * SUB_STEPS + s, 0)
    out = pl.pallas_call(
        functools.partial(_out_kernel, n_tiles=n_tiles),
        grid=(n_tiles + 1, SUB_STEPS),
        in_specs=[pl.BlockSpec((SUB_TILE, D_MODEL), lambda t, s: (nat_row(t, s), 0)),
                  pl.BlockSpec((POS_PER_STEP, D_SSM, TILE_CHUNKS), cm_idx),
                  pl.BlockSpec((POS_PER_STEP, D_SSM, TILE_CHUNKS),
                               lambda t, s: (cm_idx(t, s)[0], 1, cm_idx(t, s)[2])),
                  pl.BlockSpec((SUB_TILE, D_ATTN), lambda t, s: (nat_row(t, s), 0)),
                  pl.BlockSpec((SUB_TILE, D_ATTN), lambda t, s: (nat_row(t, s), 3)),
                  pl.BlockSpec((D_SSM, D_SSM), lambda t, s: (0, 0)),
                  colv(D_SSM), colv(D_SSM), rowv(D_ATTN),
                  pl.BlockSpec((D_MODEL, D_MODEL), lambda t, s: (0, 0)),
                  rowv(D_MODEL)],
        out_specs=pl.BlockSpec((SUB_TILE, D_MODEL), lambda t, s: (nat_row(t, s), 0)),
        out_shape=jax.ShapeDtypeStruct((n, D_MODEL), F32),
        scratch_shapes=[pltpu.VMEM((2, D_SSM // LANES, TOKEN_TILE, LANES), F32)],
        compiler_params=_params(dimension_semantics=("arbitrary", "arbitrary")),
        name="out_proj",
    )(x2, y_cm, cm, o_attn, nat, w_glu_t, b_glu, gs, ga, w_out, gfin)
    return out.reshape(bsz, seq, D_MODEL)


def kernel(x_prompt, x_sample, norm_g, w_in, lam_re, lam_im, b_re, b_im, c_re, c_im, log_dt,
           d_skip, w_glu, b_glu, rpb, ssm_out_g, attn_out_g, w_out, final_norm_g):
    assert norm_g.shape[0] == 1, "single layer only"
    mt, wsi, wso, at = _s5_tables(lam_re[0], lam_im[0], b_re[0], b_im[0], c_re[0], c_im[0], log_dt[0])
    tabs = (norm_g[0][None], w_in[0].astype(BF16), mt, wsi, wso, at, _bias_table(rpb[0]),
            _col(d_skip[0]).reshape(N_PAIRS, 2 * SSM_GROUP, LANES),
            (0.5 * w_glu[0]).astype(BF16).T, _col(0.5 * b_glu[0]), _col(ssm_out_g[0]),
            attn_out_g[0][None], w_out[0].astype(BF16), final_norm_g[None])
    return _trunk(x_prompt, tabs), _trunk(x_sample, tabs)
```

```python
import functools

import jax
import jax.numpy as jnp
import numpy as np
from jax import lax
from jax.experimental import pallas as pl
from jax.experimental.pallas import tpu as pltpu

F32 = jnp.float32
BF16 = jnp.bfloat16

D_MODEL = 1024
D_SSM = 512
SSM_GROUP = 16
N_GROUPS = D_SSM // SSM_GROUP
N_PAIRS = N_GROUPS // 2
STATE_P = 64
N_HEADS = 8
HEAD_DIM = 64
D_ATTN = N_HEADS * HEAD_DIM
D_NAT = 3 * D_ATTN + D_ATTN
D_CM = 2 * D_SSM
GRID_W = 64
WIN_H = 8
WIN_W = 16
EPS = 1e-6
CHUNK = 16
PW = 2 * CHUNK * SSM_GROUP
NEG = -1e30

LANES = 128
TILE_CHUNKS = LANES
TOKEN_TILE = TILE_CHUNKS * CHUNK
SUB_STEPS = 4
SUB_TILE = TOKEN_TILE // SUB_STEPS
POS_PER_STEP = CHUNK // SUB_STEPS
IN_SUB_STEPS = 2
IN_SUB_TILE = TOKEN_TILE // IN_SUB_STEPS
IN_POS_PER_STEP = CHUNK // IN_SUB_STEPS
N_SLABS = D_MODEL // LANES
SCAN_PAIRS = 8
OUT_PAIRS = 2
TABLE_PAIRS = 4
ATTN_ROWS_PER_STEP = 32
VMEM_LIMIT = 56 * 1024 * 1024


def _rms(x, g):
    return x * lax.rsqrt(jnp.mean(x * x, axis=-1, keepdims=True) + EPS) * g


_GELU_C0 = float(np.sqrt(2.0 / np.pi))
_GELU_C1 = 0.044715 * _GELU_C0


def _times_sigmoid(y, half_g):
    h = 0.5 * y
    return h + h * jnp.tanh(half_g)


def _silu(z):
    h = 0.5 * z
    return h + h * jnp.tanh(h)


def _gelu_tanh(x):
    h = 0.5 * x
    return h + h * jnp.tanh(x * (_GELU_C0 + _GELU_C1 * (x * x)))


def _s5_tables(lam_re, lam_im, b_re, b_im, c_re, c_im, log_dt):
    T, G, P, C = CHUNK, N_GROUPS, STATE_P, SSM_GROUP
    dt = jnp.exp(log_dt)[..., None]
    xr, xi = lam_re * dt, lam_im * dt
    n = jnp.arange(T + 1, dtype=F32)[:, None, None, None]
    mag = jnp.exp(n * xr)
    pr, pi = mag * jnp.cos(n * xi), mag * jnp.sin(n * xi)
    a_re, a_im = pr[1], pi[1]
    den = lam_re * lam_re + lam_im * lam_im
    co_re = ((a_re - 1.0) * lam_re + a_im * lam_im) / den
    co_im = (a_im * lam_re - (a_re - 1.0) * lam_im) / den
    bb_re = co_re[..., None] * b_re - co_im[..., None] * b_im
    bb_im = co_re[..., None] * b_im + co_im[..., None] * b_re

    def lanes_gp(w):
        w = w.reshape(2, N_PAIRS, 2, w.shape[2], P)
        return jnp.transpose(w, (0, 1, 3, 2, 4)).reshape(2, N_PAIRS, w.shape[3], 2 * P)

    pw_re = lanes_gp(jnp.transpose(pr, (1, 2, 0, 3)))
    pw_im = lanes_gp(jnp.transpose(pi, (1, 2, 0, 3)))
    bt_re = lanes_gp(jnp.transpose(bb_re, (0, 1, 3, 2)))
    bt_im = lanes_gp(jnp.transpose(bb_im, (0, 1, 3, 2)))
    ct_re, ct_im = lanes_gp(c_re), lanes_gp(c_im)

    small = lambda rows: pl.BlockSpec((2, TABLE_PAIRS, rows, 2 * P), lambda p: (0, p, 0, 0))
    table = pl.BlockSpec((TABLE_PAIRS, PW, PW), lambda p: (p, 0, 0))
    mt, wsi, wso = pl.pallas_call(
        _tables_kernel,
        grid=(N_PAIRS // TABLE_PAIRS,),
        in_specs=[small(T + 1), small(T + 1), small(C), small(C), small(C), small(C)],
        out_specs=[table, table, table],
        out_shape=[jax.ShapeDtypeStruct((N_PAIRS, PW, PW), BF16)] * 3,
        name="s5_tables",
    )(pw_re, pw_im, bt_re, bt_im, ct_re, ct_im)

    at = jnp.stack([pr[T, 0], pi[T, 0], pr[T, 1], pi[T, 1]], axis=0)
    at = at.reshape(4, N_PAIRS, 2 * P).transpose(1, 0, 2).reshape(N_PAIRS, 8 * P)
    return mt, wsi, wso, at


def _tables_kernel(*refs):
    for q in range(TABLE_PAIRS):
        _tables_one_pair(*[r.at[:, q] for r in refs[:6]], *[r.at[q] for r in refs[6:]])


def _tables_one_pair(pwr_ref, pwi_ref, btr_ref, bti_ref, ctr_ref, cti_ref, mt_ref, wsi_ref, wso_ref):
    T, R = CHUNK, 2 * SSM_GROUP
    first = lax.broadcasted_iota(jnp.int32, (SSM_GROUP, 2 * STATE_P), 1) < STATE_P

    def rows32(ref, d):
        w = ref[d]
        zero = jnp.zeros_like(w)
        return jnp.concatenate([jnp.where(first, w, zero), jnp.where(first, zero, w)], axis=0)

    b_re = [rows32(btr_ref, d) for d in range(2)]
    b_im = [rows32(bti_ref, d) for d in range(2)]
    c_re = [rows32(ctr_ref, d) for d in range(2)]
    c_im = [rows32(cti_ref, d) for d in range(2)]
    pw = lambda d, n: (pwr_ref[d, n:n + 1, :], pwi_ref[d, n:n + 1, :])

    def cmul(w_re, w_im, d, n):
        ar, ai = pw(d, n)
        return w_re * ar - w_im * ai, w_re * ai + w_im * ar

    for j in range(T):
        f_re, f_im = cmul(b_re[0], b_im[0], 0, T - 1 - j)
        g_re, g_im = cmul(b_re[1], b_im[1], 1, j)
        for part, w in enumerate((f_re, f_im, g_re, g_im)):
            wsi_ref[j * R:(j + 1) * R, part * LANES:(part + 1) * LANES] = w.astype(BF16)

    for i in range(T):
        f_re, f_im = cmul(c_re[0], c_im[0], 0, i + 1)
        g_re, g_im = cmul(c_re[1], c_im[1], 1, T - i)
        for part, w in enumerate((f_re, -f_im, g_re, -g_im)):
            wso_ref[i * R:(i + 1) * R, part * LANES:(part + 1) * LANES] = w.astype(BF16)

    zero = jnp.zeros((R, 2 * STATE_P), F32)
    ca_f = [cmul(c_re[0], c_im[0], 0, n) for n in range(T)]
    ca_b = [cmul(c_re[1], c_im[1], 1, n) for n in range(T)]
    lag_rows = []
    for m in range(2 * T):
        f = ca_f[m - (T - 1)] if T - 1 <= m <= 2 * T - 2 else (zero, zero)
        b = ca_b[(T - 1) - m] if m <= T - 1 else (zero, zero)
        lag_rows.append(jnp.concatenate([f[0], f[1], b[0], b[1]], axis=1))
    ca_cat = jnp.concatenate(lag_rows, axis=0)
    b_cat = jnp.concatenate([b_re[0], -b_im[0], b_re[1], -b_im[1]], axis=1)
    def hi_lo(x):
        hi = x.astype(BF16)
        return hi, (x - hi.astype(F32)).astype(BF16)

    nt = (((1,), (1,)), ((), ()))
    b_hi, b_lo = hi_lo(b_cat)
    c_hi, c_lo = hi_lo(ca_cat)
    both = lax.dot_general(jnp.concatenate([b_hi, b_lo], axis=0), c_hi, nt, preferred_element_type=F32)
    kl = both[:R] + both[R:] + lax.dot_general(b_hi, c_lo, nt, preferred_element_type=F32)
    mt_t = jnp.concatenate([kl[:, (T - 1 - j) * R:(T - 1 - j) * R + PW] for j in range(T)], axis=0)
    mt_ref[...] = mt_t.T.astype(BF16)


def _bias_table(rpb):
    nrf = 2 * WIN_H - 2
    nr, nc_ = rpb.shape[1], rpb.shape[2]
    rpb_lanes = jnp.pad(rpb.astype(F32), ((0, 0), (0, 0), (0, LANES - nc_)))
    return pl.pallas_call(
        _bias_kernel,
        grid=(N_HEADS // 2,),
        in_specs=[pl.BlockSpec((2, nr, LANES), lambda p: (p, 0, 0))],
        out_specs=pl.BlockSpec((None, nrf, 2 * GRID_W, 2 * GRID_W), lambda p: (p, 0, 0, 0)),
        out_shape=jax.ShapeDtypeStruct((N_HEADS // 2, nrf, 2 * GRID_W, 2 * GRID_W), F32),
        name="attn_bias",
    )(rpb_lanes)


def _bias_kernel(rpb_ref, out_ref):
    qc = lax.broadcasted_iota(jnp.int32, (GRID_W, LANES), 0)
    kc = lax.broadcasted_iota(jnp.int32, (GRID_W, LANES), 1)
    q_start = jnp.clip(qc - WIN_W // 2, 0, GRID_W - WIN_W)
    valid = (kc >= q_start) & (kc < q_start + WIN_W)
    left = kc < GRID_W
    nrf = out_ref.shape[0]
    for hl in range(2):
        toe, toe_right = [], []
        for r in range(nrf + 1):
            row = jnp.broadcast_to(rpb_ref[hl, r:r + 1, :], (GRID_W, LANES))
            t = pltpu.roll(row, LANES - (WIN_W - 1), 1, stride=1, stride_axis=0)
            t = jnp.where(valid, t, NEG)
            toe.append(t)
            toe_right.append(pltpu.roll(t, GRID_W, 1))
        for rf in range(nrf):
            out_ref[rf, hl * GRID_W:(hl + 1) * GRID_W, :] = jnp.where(left, toe[rf], toe_right[rf + 1])


def _col(v):
    return jnp.broadcast_to(v.astype(F32)[:, None], (v.shape[0], LANES))


def _in_proj_kernel(*refs):
    xs = refs[:N_SLABS]
    g_ref, w_ref, nat_ref, cm_ref, hp_scr = refs[N_SLABS:]
    s = pl.program_id(1)
    g = g_ref[...]
    r0 = pl.multiple_of(s * IN_SUB_TILE, IN_SUB_TILE)
    xn = jnp.concatenate([x[pl.ds(r0, IN_SUB_TILE), :] for x in xs], axis=1)
    h = _rms(xn, g).astype(BF16)
    for jl in range(IN_POS_PER_STEP):
        j = s * IN_POS_PER_STEP + jl
        xj = jnp.concatenate([x[pl.ds(j, TILE_CHUNKS, stride=CHUNK), :] for x in xs], axis=1)
        hp_scr[jl * TILE_CHUNKS:(jl + 1) * TILE_CHUNKS, :] = _rms(xj, g).astype(BF16)
    nat_ref[...] = jnp.dot(h, w_ref[:, D_CM:], preferred_element_type=F32).astype(BF16)
    cm = lax.dot_general(w_ref[:, :D_CM], hp_scr[...], (((0,), (1,)), ((), ())), preferred_element_type=F32)
    for jl in range(IN_POS_PER_STEP):
        cm_ref[jl] = cm[:, jl * TILE_CHUNKS:(jl + 1) * TILE_CHUNKS].astype(BF16)


def _scan_kernel(u_ref, w_ref, a_ref, h_ref, sin, sout, *, cps):
    pitch = cps + 8
    pair_rows = 2 * SSM_GROUP
    for pr in range(SCAN_PAIRS):
        z = u_ref[:, pr * pair_rows:(pr + 1) * pair_rows, :].reshape(PW, cps)
        st = lax.dot_general(z, w_ref[pr], (((0,), (0,)), ((), ())), preferred_element_type=F32)
        for part in range(4):
            slab = part * SCAN_PAIRS + pr
            sin[pl.ds(slab * pitch, cps), :] = st[:, part * LANES:(part + 1) * LANES]

    ar_f, ai_f = a_ref[:, 0:128], a_ref[:, 128:256]
    ar_b, ai_b = a_ref[:, 256:384], a_ref[:, 384:512]
    rows = lambda part, k: pl.ds(part * SCAN_PAIRS * pitch + k, SCAN_PAIRS, stride=pitch)

    def body(k, carry):
        hfr, hfi, hbr, hbi = carry
        kb = cps - 1 - k
        sout[rows(0, k), :] = hfr
        sout[rows(1, k), :] = hfi
        sout[rows(2, kb), :] = hbr
        sout[rows(3, kb), :] = hbi
        nfr = ar_f * hfr - ai_f * hfi + sin[rows(0, k), :]
        nfi = ar_f * hfi + ai_f * hfr + sin[rows(1, k), :]
        nbr = ar_b * hbr - ai_b * hbi + sin[rows(2, kb), :]
        nbi = ar_b * hbi + ai_b * hbr + sin[rows(3, kb), :]
        return nfr, nfi, nbr, nbi

    z = jnp.zeros((SCAN_PAIRS, LANES), F32)
    lax.fori_loop(0, cps, body, (z, z, z, z))

    for pr in range(SCAN_PAIRS):
        for part in range(4):
            slab = part * SCAN_PAIRS + pr
            lane0 = (pr * 4 + part) * LANES
            h_ref[:, lane0:lane0 + LANES] = sout[pl.ds(slab * pitch, cps), :].astype(BF16)


def _s5_out_kernel(u_ref, h_ref, mt_ref, wso_ref, d_ref, y_ref):
    nc = u_ref.shape[-1]
    pair_rows = 2 * SSM_GROUP
    nt = (((1,), (1,)), ((), ()))
    zs, ys = [], []
    for q in range(OUT_PAIRS):
        z = u_ref[:, q * pair_rows:(q + 1) * pair_rows, :].reshape(PW, nc)
        y = jnp.dot(mt_ref[q], z, preferred_element_type=F32)
        y = y + lax.dot_general(wso_ref[q], h_ref[:, q * PW:(q + 1) * PW], nt, preferred_element_type=F32)
        zs.append(z)
        ys.append(y)
    for q in range(OUT_PAIRS):
        d = jnp.concatenate([d_ref[q]] * CHUNK, axis=0)
        y = _gelu_tanh(ys[q] + jnp.concatenate([d] * (nc // LANES), axis=1) * zs[q].astype(F32))
        y_ref[:, q * pair_rows:(q + 1) * pair_rows, :] = y.astype(BF16).reshape(CHUNK, pair_rows, nc)


def _attn_kernel(q_ref, k_ref, v_ref, b_ref, o_ref, *, rows):
    lane = lax.broadcasted_iota(jnp.int32, (GRID_W, 2 * HEAD_DIM), 1)
    first = lane < HEAD_DIM
    nkeys = WIN_H * GRID_W
    ones = jnp.ones((nkeys, 2 * HEAD_DIM), BF16)

    def one_row(r):
        rs = jnp.clip(r - WIN_H // 2, 0, rows - WIN_H)
        ri0 = rs - r + (WIN_H - 1)
        q0 = pl.multiple_of(r * GRID_W, GRID_W)
        k0 = pl.multiple_of(rs * GRID_W, GRID_W)
        q = q_ref[pl.ds(q0, GRID_W), :] * jnp.asarray(HEAD_DIM ** -0.5, BF16)
        zero = jnp.zeros_like(q)
        q2 = jnp.concatenate([jnp.where(first, q, zero), jnp.where(first, zero, q)], axis=0)
        kw = k_ref[pl.ds(k0, nkeys), :]
        vw = jnp.concatenate([v_ref[pl.ds(k0, nkeys), :], ones], axis=1)
        s = lax.dot_general(q2, kw, (((1,), (1,)), ((), ())), preferred_element_type=F32)
        s = s + jnp.concatenate([b_ref[0, ri0 + 2 * m] for m in range(WIN_H // 2)], axis=1)
        p = jnp.exp(s - jnp.max(s, axis=-1, keepdims=True))
        ol = jnp.dot(p.astype(BF16), vw, preferred_element_type=F32)
        o2 = ol[:, :2 * HEAD_DIM] / ol[:, 2 * HEAD_DIM:]
        o = jnp.where(first, o2[:GRID_W], o2[GRID_W:])
        o_ref[pl.ds(q0, GRID_W), :] = o.astype(BF16)

    def body(rb, carry):
        for i in range(ATTN_ROWS_PER_STEP):
            one_row(rb * ATTN_ROWS_PER_STEP + i)
        return carry

    lax.fori_loop(0, rows // ATTN_ROWS_PER_STEP, body, 0)


def _out_kernel(x_ref, y_ref, zs_ref, o_ref, za_ref, wglu_ref, bglu_ref,
                gs_ref, ga_ref, wout_ref, gfin_ref, out_ref, ys_scr, *, n_tiles):
    t = pl.program_id(0)
    s = pl.program_id(1)
    two = lambda r: jnp.concatenate([r[...], r[...]], axis=1)

    @pl.when(t < n_tiles)
    def _():
        slot = t % 2
        for i0 in range(0, POS_PER_STEP, 2):
            cat = lambda ref: jnp.concatenate([ref[i0], ref[i0 + 1]], axis=1)
            yb = cat(y_ref)
            half_gate = jnp.dot(wglu_ref[...], yb, preferred_element_type=F32) + two(bglu_ref)
            y = _times_sigmoid(yb.astype(F32), half_gate)
            y = y * lax.rsqrt(jnp.mean(y * y, axis=0, keepdims=True) + EPS) * two(gs_ref)
            y = y * _silu(cat(zs_ref).astype(F32))
            for d in range(2):
                yt = y[:, d * LANES:(d + 1) * LANES].T
                pos = s * POS_PER_STEP + i0 + d
                for sl in range(D_SSM // LANES):
                    ys_scr[slot, sl, pl.ds(pos, TILE_CHUNKS, stride=CHUNK), :] = yt[:, sl * LANES:(sl + 1) * LANES]

    @pl.when(t > 0)
    def _():
        slot = (t + 1) % 2
        r0 = pl.multiple_of(s * SUB_TILE, SUB_TILE)
        ys = jnp.concatenate([ys_scr[slot, sl, pl.ds(r0, SUB_TILE), :] for sl in range(D_SSM // LANES)], axis=1)
        ya = _rms(o_ref[...].astype(F32), ga_ref[...]) * _silu(za_ref[...].astype(F32))
        mixed = jnp.concatenate([ys.astype(BF16), ya.astype(BF16)], axis=-1)
        out = x_ref[...] + jnp.dot(mixed, wout_ref[...], preferred_element_type=F32)
        out_ref[...] = _rms(out, gfin_ref[...])


def _params(**kw):
    return pltpu.CompilerParams(vmem_limit_bytes=VMEM_LIMIT, **kw)


def _trunk(x, tabs):
    (norm_g, w_all, mt, wsi, wso, at, bias, d_skip, w_glu_t, b_glu, gs, ga, w_out, gfin) = tabs
    bsz, seq, _ = x.shape
    n = bsz * seq
    nc = n // CHUNK
    cps = seq // CHUNK
    rows = seq // GRID_W
    x2 = x.reshape(n, D_MODEL)
    n_tiles = n // TOKEN_TILE

    nat, cm = pl.pallas_call(
        _in_proj_kernel,
        grid=(n_tiles, IN_SUB_STEPS),
        in_specs=[pl.BlockSpec((TOKEN_TILE, LANES), functools.partial(lambda sl, t, s: (t, sl), sl))
                  for sl in range(N_SLABS)]
                 + [pl.BlockSpec((1, D_MODEL), lambda t, s: (0, 0)),
                    pl.BlockSpec((D_MODEL, D_CM + D_NAT), lambda t, s: (0, 0))],
        out_specs=[pl.BlockSpec((IN_SUB_TILE, D_NAT), lambda t, s: (t * IN_SUB_STEPS + s, 0)),
                   pl.BlockSpec((IN_POS_PER_STEP, D_CM, TILE_CHUNKS), lambda t, s: (s, 0, t))],
        out_shape=[jax.ShapeDtypeStruct((n, D_NAT), BF16),
                   jax.ShapeDtypeStruct((CHUNK, D_CM, nc), BF16)],
        scratch_shapes=[pltpu.VMEM((IN_POS_PER_STEP * TILE_CHUNKS, D_MODEL), BF16)],
        compiler_params=_params(),
        name="in_proj",
    )(*([x2] * N_SLABS), norm_g, w_all)

    pair_rows = 2 * SSM_GROUP
    h_in = pl.pallas_call(
        functools.partial(_scan_kernel, cps=cps),
        grid=(N_PAIRS // SCAN_PAIRS, bsz),
        in_specs=[pl.BlockSpec((CHUNK, SCAN_PAIRS * pair_rows, cps), lambda m, b: (0, m, b)),
                  pl.BlockSpec((SCAN_PAIRS, PW, PW), lambda m, b: (m, 0, 0)),
                  pl.BlockSpec((SCAN_PAIRS, PW), lambda m, b: (m, 0))],
        out_specs=pl.BlockSpec((cps, SCAN_PAIRS * PW), lambda m, b: (b, m)),
        out_shape=jax.ShapeDtypeStruct((nc, N_PAIRS * PW), BF16),
        scratch_shapes=[pltpu.VMEM((4 * SCAN_PAIRS * (cps + 8), LANES), F32)] * 2,
        compiler_params=_params(),
        name="s5_scan",
    )(cm, wsi, at)

    y_cm = pl.pallas_call(
        _s5_out_kernel,
        grid=(N_PAIRS // OUT_PAIRS,),
        in_specs=[pl.BlockSpec((CHUNK, OUT_PAIRS * pair_rows, nc), lambda p: (0, p, 0)),
                  pl.BlockSpec((nc, OUT_PAIRS * PW), lambda p: (0, p)),
                  pl.BlockSpec((OUT_PAIRS, PW, PW), lambda p: (p, 0, 0)),
                  pl.BlockSpec((OUT_PAIRS, PW, PW), lambda p: (p, 0, 0)),
                  pl.BlockSpec((OUT_PAIRS, pair_rows, LANES), lambda p: (p, 0, 0))],
        out_specs=pl.BlockSpec((CHUNK, OUT_PAIRS * pair_rows, nc), lambda p: (0, p, 0)),
        out_shape=jax.ShapeDtypeStruct((CHUNK, D_SSM, nc), BF16),
        compiler_params=_params(),
        name="s5_out",
    )(cm, h_in, mt, wso, d_skip)

    hp = 2 * HEAD_DIM
    o_attn = pl.pallas_call(
        functools.partial(_attn_kernel, rows=rows),
        grid=(N_HEADS // 2, bsz),
        in_specs=[pl.BlockSpec((seq, hp), lambda p, b: (b, p)),
                  pl.BlockSpec((seq, hp), lambda p, b: (b, 4 + p)),
                  pl.BlockSpec((seq, hp), lambda p, b: (b, 8 + p)),
                  pl.BlockSpec((1, 2 * WIN_H - 2, 2 * GRID_W, 2 * GRID_W), lambda p, b: (p, 0, 0, 0))],
        out_specs=pl.BlockSpec((seq, hp), lambda p, b: (b, p)),
        out_shape=jax.ShapeDtypeStruct((n, D_ATTN), BF16),
        compiler_params=_params(),
        name="attention",
    )(nat, nat, nat, bias)

    colv = lambda width: pl.BlockSpec((width, LANES), lambda t, s: (0, 0))
    rowv = lambda width: pl.BlockSpec((1, width), lambda t, s: (0, 0))
    last = n_tiles - 1
    cm_idx = lambda t, s: (jnp.where(t > last, SUB_STEPS - 1, s), 0, jnp.minimum(t, last))
    nat_row = lambda t, s: jnp.where(t > 0, (t - 1) * SUB_STEPS + s, 0)
    out = pl.pallas_call(
        functools.partial(_out_kernel, n_tiles=n_tiles),
        grid=(n_tiles + 1, SUB_STEPS),
        in_specs=[pl.BlockSpec((SUB_TILE, D_MODEL), lambda t, s: (nat_row(t, s), 0)),
                  pl.BlockSpec((POS_PER_STEP, D_SSM, TILE_CHUNKS), cm_idx),
                  pl.BlockSpec((POS_PER_STEP, D_SSM, TILE_CHUNKS),
                               lambda t, s: (cm_idx(t, s)[0], 1, cm_idx(t, s)[2])),
                  pl.BlockSpec((SUB_TILE, D_ATTN), lambda t, s: (nat_row(t, s), 0)),
                  pl.BlockSpec((SUB_TILE, D_ATTN), lambda t, s: (nat_row(t, s), 3)),
                  pl.BlockSpec((D_SSM, D_SSM), lambda t, s: (0, 0)),
                  colv(D_SSM), colv(D_SSM), rowv(D_ATTN),
                  pl.BlockSpec((D_MODEL, D_MODEL), lambda t, s: (0, 0)),
                  rowv(D_MODEL)],
        out_specs=pl.BlockSpec((SUB_TILE, D_MODEL), lambda t, s: (nat_row(t, s), 0)),
        out_shape=jax.ShapeDtypeStruct((n, D_MODEL), F32),
        scratch_shapes=[pltpu.VMEM((2, D_SSM // LANES, TOKEN_TILE, LANES), F32)],
        compiler_params=_params(dimension_semantics=("arbitrary", "arbitrary")),
        name="out_proj",
    )(x2, y_cm, cm, o_attn, nat, w_glu_t, b_glu, gs, ga, w_out, gfin)
    return out.reshape(bsz, seq, D_MODEL)


def kernel(x_prompt, x_sample, norm_g, w_in, lam_re, lam_im, b_re, b_im, c_re, c_im, log_dt,
           d_skip, w_glu, b_glu, rpb, ssm_out_g, attn_out_g, w_out, final_norm_g):
    assert norm_g.shape[0] == 1, "single layer only"
    mt, wsi, wso, at = _s5_tables(lam_re[0], lam_im[0], b_re[0], b_im[0], c_re[0], c_im[0], log_dt[0])
    tabs = (norm_g[0][None], w_in[0].astype(BF16), mt, wsi, wso, at, _bias_table(rpb[0]),
            _col(d_skip[0]).reshape(N_PAIRS, 2 * SSM_GROUP, LANES),
            (0.5 * w_glu[0]).astype(BF16).T, _col(0.5 * b_glu[0]), _col(ssm_out_g[0]),
            attn_out_g[0][None], w_out[0].astype(BF16), final_norm_g[None])
    return _trunk(x_prompt, tabs), _trunk(x_sample, tabs)
```
